```python
import math
import jax, jax.numpy as jnp
from jax import lax
import numpy as np

D_MODEL = 1024
BATCH = 1
SEQ = 16384
DEPTH = 1
DEC_BATCH = 32
DEC_SEQ = 2048
PAST_LEN = 128

N_META = 16
CHUNK = 128
NPAD = CHUNK - N_META
N_ATT_HEADS = 8
ATT_DH = 64
ATT_VDIM = N_ATT_HEADS * 2 * ATT_DH
SSM_HD = 64
D_SSM = D_MODEL
H_SSM = D_SSM // SSM_HD
SSM_G = 2
SSM_R = H_SSM // SSM_G
SSM_N = 128
D_CONV = 5
CONV_DIM = D_SSM + 2 * SSM_G * SSM_N
MIX_DIM = ATT_VDIM + D_SSM
SPLIT_SIZES = (ATT_VDIM, ATT_VDIM, ATT_VDIM, D_SSM, CONV_DIM, 2 * H_SSM)
IN_DIM = sum(SPLIT_SIZES)
D_FF = 4 * D_MODEL
EPS = 1e-5

kernel_name = "hymba_diffattn_bissd_encoder"


def rmsnorm(x, w):
    xf = x.astype(jnp.float32)
    y = xf * lax.rsqrt(jnp.mean(xf * xf, axis=-1, keepdims=True) + EPS)
    return (y * w.astype(jnp.float32)).astype(x.dtype)


def pad_front(t):
    return jnp.pad(t, [(0, 0), (NPAD, 0)] + [(0, 0)] * (t.ndim - 2))


def alibi_slopes():
    return 2.0 ** (-8.0 * (jnp.arange(N_ATT_HEADS, dtype=jnp.float32) + 1.0) / N_ATT_HEADS)


def dwconv_centred(x, w, b):
    kern = w[:, None, :].astype(x.dtype)
    y = lax.conv_general_dilated(x, kern, window_strides=(1,), padding=[(D_CONV // 2, D_CONV // 2)],
                                 dimension_numbers=('NWC', 'WIO', 'NWC'), feature_group_count=x.shape[-1])
    return y + b.astype(x.dtype)


def diff_attention(q, k, v, lam, norm_w, lam_init):
    b, L, _ = q.shape
    P = L + NPAD
    nb = P // CHUNK
    qh = pad_front(q).reshape(b, P, N_ATT_HEADS, 2, ATT_DH)
    kh = pad_front(k).reshape(b, P, N_ATT_HEADS, 2, ATT_DH)
    vh = pad_front(v).reshape(b, P, N_ATT_HEADS, 2 * ATT_DH)
    idx = jnp.arange(P)
    kpos = (idx - NPAD).astype(jnp.float32)
    kvalid = idx >= NPAD
    slopes = alibi_slopes()
    scale = ATT_DH ** -0.5
    qb = jnp.swapaxes(qh.reshape(b, nb, CHUNK, N_ATT_HEADS, 2, ATT_DH), 0, 1)
    qpos = kpos.reshape(nb, CHUNK)

    def block(args):
        qblk, qp = args
        s = jnp.einsum('bqhjd,bkhjd->bhjqk', qblk, kh, preferred_element_type=jnp.float32) * scale
        bias = -slopes[:, None, None] * jnp.abs(qp[:, None] - kpos[None, :])
        s = jnp.where(kvalid, s + bias[None, :, None], -1e30)
        a = jax.nn.softmax(s, axis=-1)
        pm = a[:, :, 0] - lam * a[:, :, 1]
        return jnp.einsum('bhqk,bkhe->bqhe', pm.astype(vh.dtype), vh)

    o = lax.map(block, (qb, qpos))
    o = jnp.swapaxes(o, 0, 1).reshape(b, P, N_ATT_HEADS, 2 * ATT_DH)[:, NPAD:]
    o = rmsnorm(o, norm_w) * (1.0 - lam_init)
    return o.reshape(b, L, ATT_VDIM)


def ssd_scan(x, dt, a, bm, cm):
    b, P = x.shape[:2]
    c = P // CHUNK
    xg = (x * dt[..., None]).reshape(b, c, CHUNK, SSM_G, SSM_R, SSM_HD)
    adt = (dt * a).reshape(b, c, CHUNK, SSM_G, SSM_R).transpose(0, 3, 4, 1, 2)
    bc = bm.reshape(b, c, CHUNK, SSM_G, SSM_N)
    cc = cm.reshape(b, c, CHUNK, SSM_G, SSM_N)
    acs = jnp.cumsum(adt, axis=-1)
    lower = jnp.tril(jnp.ones((CHUNK, CHUNK), dtype=bool))
    seg = acs[..., :, None] - acs[..., None, :]
    lmat = jnp.where(lower, jnp.exp(jnp.where(lower, seg, 0.0)), 0.0)
    y_diag = jnp.einsum('bclgn,bcsgn,bgrcls,bcsgrp->bclgrp', cc, bc, lmat, xg)
    decay_states = jnp.exp(acs[..., -1:] - acs)
    states = jnp.einsum('bcsgn,bgrcs,bcsgrp->bcgrpn', bc, decay_states, xg)
    chunk_decay = jnp.exp(acs[..., -1])

    def step(carry, inp):
        st, dec = inp
        return carry * dec[..., None, None] + st, carry

    init = jnp.zeros((b, SSM_G, SSM_R, SSM_HD, SSM_N), jnp.float32)
    _, prev = lax.scan(step, init, (jnp.moveaxis(states, 1, 0), jnp.moveaxis(chunk_decay, 3, 0)))
    prev = jnp.moveaxis(prev, 0, 1)
    y_off = jnp.einsum('bclgn,bcgrpn,bgrcl->bclgrp', cc, prev, jnp.exp(acs))
    return (y_diag + y_off).reshape(b, P, H_SSM, SSM_HD)


def bissd_mixer(xs, bm, cm, z, dt_raw, dt_bias_f, dt_bias_b, a_log_f, a_log_b, d_skip, norm_w):
    b, L, _ = xs.shape
    P = L + NPAD
    f32 = jnp.float32
    xh = pad_front(xs.astype(f32)).reshape(b, P, H_SSM, SSM_HD)
    bmp = pad_front(bm.astype(f32)).reshape(b, P, SSM_G, SSM_N)
    cmp_ = pad_front(cm.astype(f32)).reshape(b, P, SSM_G, SSM_N)
    dtr = dt_raw.astype(f32)
    dt_f = pad_front(jax.nn.softplus(dtr[..., :H_SSM] + dt_bias_f.astype(f32)))
    dt_b = pad_front(jax.nn.softplus(dtr[..., H_SSM:] + dt_bias_b.astype(f32)))
    a_f = -jnp.exp(a_log_f.astype(f32))
    a_b = -jnp.exp(a_log_b.astype(f32))
    flip = lambda t: jnp.flip(t, axis=1)
    y_f = ssd_scan(xh, dt_f, a_f, bmp, cmp_)
    y_b = flip(ssd_scan(flip(xh), flip(dt_b), a_b, flip(bmp), flip(cmp_)))
    y = y_f + y_b + xh * d_skip.astype(f32)[:, None]
    y = y[:, NPAD:].reshape(b, L, D_SSM)
    return rmsnorm(y * jax.nn.silu(z.astype(f32)), norm_w).astype(z.dtype)


def trunk(x, meta_tokens, norm1_w, w_in, conv_w, conv_b, lambda_q1, lambda_k1, lambda_q2, lambda_k2,
          attn_norm_w, dt_bias_f, dt_bias_b, a_log_f, a_log_b, d_skip, ssm_norm_w, w_out,
          norm2_w, w_up, w_down, final_norm_w):
    b = x.shape[0]
    meta = jnp.broadcast_to(meta_tokens.astype(x.dtype)[None], (b, N_META, D_MODEL))
    h = jnp.concatenate([meta, x], axis=1)
    cuts = list(np.cumsum(SPLIT_SIZES)[:-1])
    for li in range(DEPTH):
        lam_init = 0.8 - 0.6 * math.exp(-0.3 * li)
        u = rmsnorm(h, norm1_w[li])
        proj = u @ w_in[li]
        q, k, v, z, xbc, dt_raw = jnp.split(proj, cuts, axis=-1)
        xbc = jax.nn.silu(dwconv_centred(xbc, conv_w[li], conv_b[li]))
        xs, bm, cm = jnp.split(xbc, [D_SSM, D_SSM + SSM_G * SSM_N], axis=-1)
        lam = (jnp.exp(jnp.sum(lambda_q1[li].astype(jnp.float32) * lambda_k1[li].astype(jnp.float32)))
               - jnp.exp(jnp.sum(lambda_q2[li].astype(jnp.float32) * lambda_k2[li].astype(jnp.float32)))
               + lam_init)
        att = diff_attention(q, k, v, lam, attn_norm_w[li], lam_init)
        ssm = bissd_mixer(xs, bm, cm, z, dt_raw, dt_bias_f[li], dt_bias_b[li], a_log_f[li], a_log_b[li],
                          d_skip[li], ssm_norm_w[li])
        h = h + jnp.concatenate([att, ssm.astype(att.dtype)], axis=-1) @ w_out[li]
        u2 = rmsnorm(h, norm2_w[li])
        h = h + jnp.square(jax.nn.relu(u2 @ w_up[li])) @ w_down[li]
    h = rmsnorm(h, final_norm_w)
    return h[:, N_META:]


def setup_inputs(seed: int = 0) -> dict:
    key = jax.random.key(seed)
    ks = jax.random.split(key, 24)
    f32 = jnp.float32
    nrm = lambda k, shape, s: jax.random.normal(k, shape, f32) * s
    dt0 = jnp.exp(jax.random.uniform(ks[12], (DEPTH, H_SSM), f32, math.log(1e-3), math.log(1e-1)))
    dt1 = jnp.exp(jax.random.uniform(ks[13], (DEPTH, H_SSM), f32, math.log(1e-3), math.log(1e-1)))
    inv_sp = lambda d: d + jnp.log(-jnp.expm1(-d))
    return {
        "x_prompt": nrm(ks[0], (BATCH, SEQ, D_MODEL), 1.0),
        "x_sample": nrm(ks[1], (DEC_BATCH, DEC_SEQ, D_MODEL), 1.0),
        "meta_tokens": nrm(ks[2], (N_META, D_MODEL), 1.0),
        "norm1_w": 1.0 + nrm(ks[3], (DEPTH, D_MODEL), 0.02),
        "w_in": nrm(ks[4], (DEPTH, D_MODEL, IN_DIM), D_MODEL ** -0.5),
        "conv_w": nrm(ks[5], (DEPTH, D_CONV, CONV_DIM), D_CONV ** -0.5),
        "conv_b": nrm(ks[6], (DEPTH, CONV_DIM), 0.02),
        "lambda_q1": nrm(ks[7], (DEPTH, ATT_DH), 0.1),
        "lambda_k1": nrm(ks[8], (DEPTH, ATT_DH), 0.1),
        "lambda_q2": nrm(ks[9], (DEPTH, ATT_DH), 0.1),
        "lambda_k2": nrm(ks[10], (DEPTH, ATT_DH), 0.1),
        "attn_norm_w": 1.0 + nrm(ks[11], (DEPTH, 2 * ATT_DH), 0.02),
        "dt_bias_f": inv_sp(dt0),
        "dt_bias_b": inv_sp(dt1),
        "a_log_f": jnp.log(jax.random.uniform(ks[14], (DEPTH, H_SSM), f32, 1.0, 16.0)),
        "a_log_b": jnp.log(jax.random.uniform(ks[15], (DEPTH, H_SSM), f32, 1.0, 16.0)),
        "d_skip": 1.0 + nrm(ks[16], (DEPTH, H_SSM), 0.02),
        "ssm_norm_w": 1.0 + nrm(ks[17], (DEPTH, D_SSM), 0.02),
        "w_out": nrm(ks[18], (DEPTH, MIX_DIM, D_MODEL), MIX_DIM ** -0.5),
        "norm2_w": 1.0 + nrm(ks[19], (DEPTH, D_MODEL), 0.02),
        "w_up": nrm(ks[20], (DEPTH, D_MODEL, D_FF), D_MODEL ** -0.5),
        "w_down": nrm(ks[21], (DEPTH, D_FF, D_MODEL), D_FF ** -0.5),
        "final_norm_w": 1.0 + nrm(ks[22], (D_MODEL,), 0.02),
    }


def reference(x_prompt, x_sample, meta_tokens, norm1_w, w_in, conv_w, conv_b, lambda_q1, lambda_k1,
              lambda_q2, lambda_k2, attn_norm_w, dt_bias_f, dt_bias_b, a_log_f, a_log_b, d_skip,
              ssm_norm_w, w_out, norm2_w, w_up, w_down, final_norm_w):
    params = (meta_tokens, norm1_w, w_in, conv_w, conv_b, lambda_q1, lambda_k1, lambda_q2, lambda_k2,
              attn_norm_w, dt_bias_f, dt_bias_b, a_log_f, a_log_b, d_skip, ssm_norm_w, w_out,
              norm2_w, w_up, w_down, final_norm_w)
    y_prompt = trunk(x_prompt, *params)
    y_sample = trunk(x_sample, *params)
    return (y_prompt, y_sample)
```

```python
import functools
import math

import jax
import jax.numpy as jnp
from jax import lax
from jax.experimental import pallas as pl
from jax.experimental.pallas import tpu as pltpu

F32 = jnp.float32
BF16 = jnp.bfloat16

D_MODEL = 1024
N_META = 16
CHUNK = 128
NPAD = CHUNK - N_META
N_HEADS = 8
ATT_DH = 64
HEAD_W = 2 * ATT_DH
ATT_VDIM = N_HEADS * HEAD_W
D_SSM = 1024
H_SSM = 16
SSM_HD = 64
SSM_G = 2
SSM_N = 128
D_CONV = 5
CONV_DIM = D_SSM + 2 * SSM_G * SSM_N
D_FF = 4 * D_MODEL
EPS = 1e-5
LAM_INIT = 0.8 - 0.6 * math.exp(-0.3 * 0)
NEG = -1e30

LANES = 128
HALO = 8
VMEM_LIMIT = 56 * 1024 * 1024

ROW_TILE = 512
TQ = 256
ONES_ROWS = 16


def _dot(a, b):
    return jnp.dot(a, b, preferred_element_type=F32)


def _rms(x, w):
    return x * lax.rsqrt(jnp.mean(x * x, axis=-1, keepdims=True) + EPS) * w


def _silu(x):
    return x / (1.0 + jnp.exp(-x))


def _const_spec(shape):
    nd = len(shape)
    return pl.BlockSpec(shape, lambda *_: (0,) * nd)


def _params(sem):
    return pltpu.CompilerParams(dimension_semantics=sem, vmem_limit_bytes=VMEM_LIMIT)


def _inproj_body(x_ref, n1_ref, wq_ref, wk_ref, wv_ref, wz_ref, wx_ref, wdt_ref,
                 qT_ref, k_ref, vT_ref, z_ref, xbc_ref, dt_ref):
    u = _rms(x_ref[...], n1_ref[...]).astype(BF16)
    q = _dot(u, wq_ref[...]) * (ATT_DH ** -0.5)
    for h in range(N_HEADS):
        qT_ref[h] = q[:, h * HEAD_W:(h + 1) * HEAD_W].T.astype(BF16)
    k = _dot(u, wk_ref[...])
    for h in range(N_HEADS):
        k_ref[h] = k[:, h * HEAD_W:(h + 1) * HEAD_W].astype(BF16)
    v = _dot(u, wv_ref[...])
    for h in range(N_HEADS):
        vT_ref[h, 0] = v[:, h * HEAD_W:(h + 1) * HEAD_W].T.astype(BF16)
    z_ref[...] = _dot(u, wz_ref[...])
    xbc_ref[...] = _dot(u, wx_ref[...])
    dt_ref[...] = _dot(u, wdt_ref[...])


def _inproj(x2d, n1, wq, wk, wv, wz, wx, wdt, tm):
    rows = x2d.shape[0]
    grid = (rows // tm,)
    out_shape = (
        jax.ShapeDtypeStruct((N_HEADS, HEAD_W, rows), BF16),
        jax.ShapeDtypeStruct((N_HEADS, rows, HEAD_W), BF16),
        jax.ShapeDtypeStruct((N_HEADS, rows // tm, HEAD_W, tm), BF16),
        jax.ShapeDtypeStruct((rows, D_SSM), F32),
        jax.ShapeDtypeStruct((rows, CONV_DIM), F32),
        jax.ShapeDtypeStruct((rows, LANES), F32),
    )
    in_specs = [
        pl.BlockSpec((tm, D_MODEL), lambda i: (i, 0)),
        _const_spec((1, D_MODEL)),
        _const_spec(wq.shape), _const_spec(wk.shape), _const_spec(wv.shape),
        _const_spec(wz.shape), _const_spec(wx.shape), _const_spec(wdt.shape),
    ]
    out_specs = (
        pl.BlockSpec((N_HEADS, HEAD_W, tm), lambda i: (0, 0, i)),
        pl.BlockSpec((N_HEADS, tm, HEAD_W), lambda i: (0, i, 0)),
        pl.BlockSpec((N_HEADS, 1, HEAD_W, tm), lambda i: (0, i, 0, 0)),
        pl.BlockSpec((tm, D_SSM), lambda i: (i, 0)),
        pl.BlockSpec((tm, CONV_DIM), lambda i: (i, 0)),
        pl.BlockSpec((tm, LANES), lambda i: (i, 0)),
    )
    return pl.pallas_call(
        _inproj_body, out_shape=out_shape, grid=grid, in_specs=in_specs, out_specs=out_specs,
        compiler_params=_params(("arbitrary",)), name="inproj",
    )(x2d, n1, wq, wk, wv, wz, wx, wdt)


def _conv_taps(ext, w, b, rows):
    acc = b + w[0:1, :] * ext[HALO - 2:HALO - 2 + rows, :]
    for j in range(1, D_CONV):
        acc = acc + w[j:j + 1, :] * ext[HALO - 2 + j:HALO - 2 + j + rows, :]
    return _silu(acc)


def _conv_body(main_ref, prev_ref, next_ref, metaraw_ref, w_ref, b_ref, act_ref, metaact_ref,
               *, tc, nt):
    i = pl.program_id(1)
    main = main_ref[...]
    w = w_ref[...]
    b = b_ref[...]
    prev = jnp.where(i == 0, metaraw_ref[CHUNK - HALO:CHUNK, :], prev_ref[...])
    nxt = jnp.where(i == nt - 1, 0.0, next_ref[...])
    ext = jnp.concatenate([prev, main, nxt], axis=0)
    act_ref[...] = _conv_taps(ext, w, b, tc)

    @pl.when(i == 0)
    def _():
        zeros = jnp.zeros((HALO, CONV_DIM), F32)
        extm = jnp.concatenate([zeros, metaraw_ref[...], main[0:HALO, :]], axis=0)
        am = _conv_taps(extm, w, b, CHUNK)
        row = lax.broadcasted_iota(jnp.int32, (CHUNK, 1), 0)
        metaact_ref[0] = jnp.where(row >= NPAD, am, 0.0)


def _conv(xbc_raw, meta_raw, conv_w8, conv_b, nb, seq, tc):
    nt = seq // tc
    hb = tc // HALO
    nblk8 = (nb * seq) // HALO
    in_specs = [
        pl.BlockSpec((tc, CONV_DIM), lambda b, i: (b * nt + i, 0)),
        pl.BlockSpec((HALO, CONV_DIM), lambda b, i: (jnp.maximum((b * nt + i) * hb - 1, 0), 0)),
        pl.BlockSpec((HALO, CONV_DIM),
                     lambda b, i: (jnp.minimum((b * nt + i + 1) * hb, nblk8 - 1), 0)),
        _const_spec((CHUNK, CONV_DIM)),
        _const_spec((HALO, CONV_DIM)),
        _const_spec((1, CONV_DIM)),
    ]
    out_specs = (
        pl.BlockSpec((tc, CONV_DIM), lambda b, i: (b * nt + i, 0)),
        pl.BlockSpec((1, CHUNK, CONV_DIM), lambda b, i: (b, 0, 0)),
    )
    out_shape = (
        jax.ShapeDtypeStruct((nb * seq, CONV_DIM), F32),
        jax.ShapeDtypeStruct((nb, CHUNK, CONV_DIM), F32),
    )
    return pl.pallas_call(
        functools.partial(_conv_body, tc=tc, nt=nt), out_shape=out_shape, grid=(nb, nt),
        in_specs=in_specs, out_specs=out_specs,
        compiler_params=_params(("arbitrary", "arbitrary")), name="conv",
    )(xbc_raw, xbc_raw, xbc_raw, meta_raw, conv_w8, conv_b)


def _split3(x):
    hi = x.astype(BF16)
    r = x - hi.astype(F32)
    mid = r.astype(BF16)
    lo = (r - mid.astype(F32)).astype(BF16)
    return hi, mid, lo


def _expand_heads(x, e):
    hi = x.astype(BF16)
    lo = (x - hi.astype(F32)).astype(BF16)
    return _dot(hi, e) + _dot(lo, e)


def _ssd_chunk(act, dtr, valid, tri, e, biasv, alogv, lane_lo, st_ref, fwd):
    xs = act[:, :D_SSM]
    lane = lax.broadcasted_iota(jnp.int32, (1, LANES), 1)
    hmask = (lane >= lane_lo) & (lane < lane_lo + H_SSM)
    x = dtr + biasv
    dt = jnp.maximum(x, 0.0) + jnp.log(1.0 + jnp.exp(-jnp.abs(x)))
    dt = jnp.where(hmask & valid, dt, 0.0)
    adt = dt * (-jnp.exp(alogv))
    hi, mid, lo = _split3(adt)
    acs = _dot(tri, hi) + _dot(tri, mid) + _dot(tri, lo)
    acs_t = acs.T
    edge = acs[CHUNK - 1:CHUNK, :] if fwd else acs[0:1, :]
    dt_x = _expand_heads(dt, e)
    ein_x = _expand_heads(jnp.exp(acs), e)
    est_x = _expand_heads(jnp.exp(edge - acs), e)
    cd_x = ein_x[CHUNK - 1:CHUNK, :] if fwd else ein_x[0:1, :]
    xg = xs * dt_x
    xgd = (xg * est_x).astype(BF16)
    li = lax.broadcasted_iota(jnp.int32, (CHUNK, CHUNK), 0)
    si = lax.broadcasted_iota(jnp.int32, (CHUNK, CHUNK), 1)
    tmask = (li >= si) if fwd else (si >= li)
    left = lax.broadcasted_iota(jnp.int32, (1, LANES), 1) < SSM_HD
    ys = []
    heads_per_group = H_SSM // SSM_G
    for g in range(SSM_G):
        bg = act[:, D_SSM + g * SSM_N:D_SSM + (g + 1) * SSM_N]
        cg = act[:, D_SSM + SSM_G * SSM_N + g * SSM_N:D_SSM + SSM_G * SSM_N + (g + 1) * SSM_N]
        cb = cg.astype(BF16)
        bb = bg.astype(BF16)
        gm = lax.dot_general(cb, bb, (((1,), (1,)), ((), ())), preferred_element_type=F32)
        st = st_ref[g]
        yoff = _dot(cb, st.astype(BF16))
        for jp in range(heads_per_group // 2):
            pair = g * (heads_per_group // 2) + jp
            sl = slice(pair * LANES, (pair + 1) * LANES)
            xg_pair = xg[:, sl]
            ypair = ein_x[:, sl] * yoff[:, jp * LANES:(jp + 1) * LANES]
            for hh in range(2):
                c = lane_lo + pair * 2 + hh
                seg = acs[:, c:c + 1] - acs_t[c:c + 1, :]
                decay = jnp.exp(jnp.where(tmask, seg, NEG))
                md = (gm * decay).astype(BF16)
                half = left if hh == 0 else jnp.logical_not(left)
                ypair = ypair + _dot(md, jnp.where(half, xg_pair, 0.0).astype(BF16))
            ys.append(ypair)
        gsl = slice(g * heads_per_group * SSM_HD, (g + 1) * heads_per_group * SSM_HD)
        st_ref[g] = st * cd_x[:, gsl] + _dot(bg.T.astype(BF16), xgd[:, gsl])
    return jnp.concatenate(ys, axis=1), xs


def _ssd_fwd_body(actx_ref, actm_ref, dtx_ref, dtm_ref, tri_ref, e_ref, biasv_ref, alogv_ref,
                  yf_ref, st_ref):
    c = pl.program_id(1)

    @pl.when(c == 0)
    def _():
        st_ref[...] = jnp.zeros_like(st_ref)

    is_meta = c == 0
    act = jnp.where(is_meta, actm_ref[0], actx_ref[...])
    dtr = jnp.where(is_meta, dtm_ref[...], dtx_ref[...])
    row = lax.broadcasted_iota(jnp.int32, (CHUNK, 1), 0)
    valid = jnp.logical_or(jnp.logical_not(is_meta), row >= NPAD)
    y, _ = _ssd_chunk(act, dtr, valid, tri_ref[...], e_ref[...], biasv_ref[...], alogv_ref[...],
                      0, st_ref, True)
    yf_ref[...] = y


def _ssd_bwd_body(act_ref, dt_ref, yf_ref, z_ref, tri_ref, e_ref, biasv_ref, alogv_ref,
                  dskip_ref, nw_ref, out_ref, st_ref):
    c = pl.program_id(1)

    @pl.when(c == 0)
    def _():
        st_ref[...] = jnp.zeros_like(st_ref)

    yb, xs = _ssd_chunk(act_ref[...], dt_ref[...], True, tri_ref[...], e_ref[...],
                        biasv_ref[...], alogv_ref[...], H_SSM, st_ref, False)
    y = yf_ref[...] + yb + xs * dskip_ref[...]
    out_ref[...] = _rms(y * _silu(z_ref[...]), nw_ref[...]).astype(BF16)


def _ssd(act, meta_act, dt_raw, dt_meta, z, consts, nb, seq):
    nc = seq // CHUNK
    tril, triu, e_f, e_b, biasv, alogv, dskip_x, ssm_nw = consts
    st_shape = pltpu.VMEM((SSM_G, SSM_N, (H_SSM // SSM_G) * SSM_HD), F32)

    xblk = lambda b, c: (b * nc + jnp.maximum(c - 1, 0), 0)
    y_f = pl.pallas_call(
        _ssd_fwd_body,
        out_shape=jax.ShapeDtypeStruct((nb * seq, D_SSM), F32),
        grid=(nb, nc + 1),
        in_specs=[
            pl.BlockSpec((CHUNK, CONV_DIM), xblk),
            pl.BlockSpec((1, CHUNK, CONV_DIM), lambda b, c: (b, 0, 0)),
            pl.BlockSpec((CHUNK, LANES), xblk),
            _const_spec((CHUNK, LANES)),
            _const_spec((CHUNK, CHUNK)), _const_spec((LANES, D_SSM)),
            _const_spec((1, LANES)), _const_spec((1, LANES)),
        ],
        out_specs=pl.BlockSpec((CHUNK, D_SSM), xblk),
        scratch_shapes=[st_shape],
        compiler_params=_params(("arbitrary", "arbitrary")), name="ssd_fwd",
    )(act, meta_act, dt_raw, dt_meta, tril, e_f, biasv, alogv)

    rblk = lambda b, c: (b * nc + nc - 1 - c, 0)
    return pl.pallas_call(
        _ssd_bwd_body,
        out_shape=jax.ShapeDtypeStruct((nb * seq, D_SSM), BF16),
        grid=(nb, nc),
        in_specs=[
            pl.BlockSpec((CHUNK, CONV_DIM), rblk),
            pl.BlockSpec((CHUNK, LANES), rblk),
            pl.BlockSpec((CHUNK, D_SSM), rblk),
            pl.BlockSpec((CHUNK, D_SSM), rblk),
            _const_spec((CHUNK, CHUNK)), _const_spec((LANES, D_SSM)),
            _const_spec((1, LANES)), _const_spec((1, LANES)),
            _const_spec((1, D_SSM)), _const_spec((1, D_SSM)),
        ],
        out_specs=pl.BlockSpec((CHUNK, D_SSM), rblk),
        scratch_shapes=[st_shape],
        compiler_params=_params(("arbitrary", "arbitrary")), name="ssd_bwd",
    )(act, dt_raw, y_f, z, triu, e_b, biasv, alogv, dskip_x, ssm_nw)


def _attn_body(qT_ref, k_ref, vT_ref, km_ref, vmT_ref, slope_ref, lq1_ref, lk1_ref, lq2_ref,
               lk2_ref, nw_ref, out_ref, acc0_ref, acc1_ref, *, tq, tk, seq):
    i = pl.program_id(2)
    q0 = i * tq
    jd = q0 // tk
    nkv = seq // tk
    slope = slope_ref[0]
    slope_q = slope[:, 0:1]
    qT = qT_ref[0]
    row = lax.broadcasted_iota(jnp.int32, (HEAD_W, 1), 0)
    qz = (jnp.where(row < ATT_DH, qT, jnp.zeros_like(qT)),
          jnp.where(row >= ATT_DH, qT, jnp.zeros_like(qT)))
    frow = lax.broadcasted_iota(jnp.int32, (HEAD_W, tq), 0)
    feat_pos = jnp.where(frow < 2, 1.0, 0.0).astype(BF16)
    feat_neg = jnp.where(frow < 2, -1.0, 0.0).astype(BF16)
    q_before = tuple(jnp.concatenate([z, feat_pos], axis=0) for z in qz)
    q_after = tuple(jnp.concatenate([z, feat_neg], axis=0) for z in qz)
    kr = lax.broadcasted_iota(jnp.int32, (tk, LANES), 0)
    kl = lax.broadcasted_iota(jnp.int32, (tk, LANES), 1)
    kr_hi = (kr // 256) * 256
    kfeat = jnp.where(kl == 0, kr_hi.astype(F32), jnp.where(kl == 1, (kr - kr_hi).astype(F32), 0.0))
    kfeat = (kfeat * slope).astype(BF16)
    cq = lax.broadcasted_iota(jnp.int32, (1, tq), 1).astype(F32)
    ones_rows = jnp.ones((ONES_ROWS, tk), BF16)
    accs = (acc0_ref, acc1_ref)

    def update(sT, cterm, m_old, acc_ref, v_aug):
        mb = jnp.max(sT, axis=0, keepdims=True)
        m_new = jnp.maximum(m_old, mb + cterm)
        p = jnp.exp(sT - (m_new - cterm)).astype(BF16)
        alpha = jnp.exp(m_old - m_new)
        acc_ref[...] = acc_ref[...] * alpha + _dot(v_aug, p)
        return m_new

    mr = lax.broadcasted_iota(jnp.int32, (CHUNK, tq), 0)
    mc = lax.broadcasted_iota(jnp.int32, (CHUNK, tq), 1)
    dist_m = (q0 + N_META + mc - (mr - NPAD)).astype(F32)
    bias_m = jnp.where(mr >= NPAD, -slope_q * dist_m, NEG)
    vm_aug = jnp.concatenate([vmT_ref[0, 0], jnp.ones((ONES_ROWS, CHUNK), BF16)], axis=0)
    km = km_ref[0]
    m_run = []
    for mp in range(2):
        accs[mp][...] = jnp.zeros_like(accs[mp])
        sT = _dot(km, qz[mp]) + bias_m
        m_run.append(update(sT, 0.0, jnp.full((1, tq), NEG, F32), accs[mp], vm_aug))

    def off_diag(q_aug, sign):
        def body(j, ms):
            k0 = pl.multiple_of(j * tk, tk)
            lhs = jnp.concatenate([k_ref[0, pl.ds(k0, tk), :], kfeat], axis=1)
            v_aug = jnp.concatenate([vT_ref[0, j], ones_rows], axis=0)
            cterm = slope_q * (sign * ((k0 - q0).astype(F32) - cq))
            return tuple(update(_dot(lhs, q_aug[mp]), cterm, ms[mp], accs[mp], v_aug)
                         for mp in range(2))
        return body

    m_run = lax.fori_loop(0, jd, off_diag(q_before, 1.0), tuple(m_run))

    kd0 = pl.multiple_of(jd * tk, tk)
    dr = lax.broadcasted_iota(jnp.int32, (tk, tq), 0)
    dc = lax.broadcasted_iota(jnp.int32, (tk, tq), 1)
    bias_d = -slope_q * jnp.abs(kd0 + dr - q0 - dc).astype(F32)
    k_d = k_ref[0, pl.ds(kd0, tk), :]
    v_d = jnp.concatenate([vT_ref[0, jd], ones_rows], axis=0)
    m_run = tuple(update(_dot(k_d, qz[mp]) + bias_d, 0.0, m_run[mp], accs[mp], v_d)
                  for mp in range(2))

    m_run = lax.fori_loop(jd + 1, nkv, off_diag(q_after, -1.0), m_run)

    lam = (jnp.exp(jnp.sum(lq1_ref[...] * lk1_ref[...], axis=-1, keepdims=True))
           - jnp.exp(jnp.sum(lq2_ref[...] * lk2_ref[...], axis=-1, keepdims=True)) + LAM_INIT)
    o0 = acc0_ref[0:HEAD_W, :] / acc0_ref[HEAD_W:HEAD_W + 1, :]
    o1 = acc1_ref[0:HEAD_W, :] / acc1_ref[HEAD_W:HEAD_W + 1, :]
    o = o0 - lam * o1
    on = o * lax.rsqrt(jnp.mean(o * o, axis=0, keepdims=True) + EPS)
    out_ref[...] = ((on.T * nw_ref[...]) * (1.0 - LAM_INIT)).astype(BF16)


def _attention(qT, k, vT, k_meta, vT_meta, slopes, lq1, lk1, lq2, lk2, attn_nw, nb, seq):
    tq, tk = TQ, min(ROW_TILE, seq)
    nq = seq // tq
    grid = (N_HEADS, nb, nq)
    vec = lambda n: _const_spec((1, n))
    in_specs = [
        pl.BlockSpec((1, HEAD_W, tq), lambda h, b, i: (h, 0, b * nq + i)),
        pl.BlockSpec((1, seq, HEAD_W), lambda h, b, i: (h, b, 0)),
        pl.BlockSpec((1, seq // tk, HEAD_W, tk), lambda h, b, i: (h, b, 0, 0)),
        pl.BlockSpec((1, CHUNK, HEAD_W), lambda h, b, i: (h, 0, 0)),
        pl.BlockSpec((1, 1, HEAD_W, CHUNK), lambda h, b, i: (h, 0, 0, 0)),
        pl.BlockSpec((1, 1, LANES), lambda h, b, i: (h, 0, 0)),
        vec(ATT_DH), vec(ATT_DH), vec(ATT_DH), vec(ATT_DH), vec(HEAD_W),
    ]
    return pl.pallas_call(
        functools.partial(_attn_body, tq=tq, tk=tk, seq=seq),
        out_shape=jax.ShapeDtypeStruct((nb * seq, ATT_VDIM), BF16),
        grid=grid, in_specs=in_specs,
        out_specs=pl.BlockSpec((tq, HEAD_W), lambda h, b, i: (b * nq + i, h)),
        scratch_shapes=[pltpu.VMEM((HEAD_W + ONES_ROWS, tq), F32),
                        pltpu.VMEM((HEAD_W + ONES_ROWS, tq), F32)],
        compiler_params=_params(("arbitrary", "arbitrary", "arbitrary")), name="attn",
    )(qT, k, vT, k_meta, vT_meta, slopes, lq1, lk1, lq2, lk2, attn_nw)


def _mlp_body(x_ref, att_ref, ssm_ref, wo_ref, n2_ref, wup_ref, wdown_ref, fw_ref, out_ref):
    h1 = (x_ref[...] + _dot(att_ref[...], wo_ref[0:ATT_VDIM, :])
          + _dot(ssm_ref[...], wo_ref[ATT_VDIM:ATT_VDIM + D_SSM, :]))
    u2 = _rms(h1, n2_ref[...]).astype(BF16)
    hid = jnp.maximum(_dot(u2, wup_ref[...]), 0.0)
    h2 = h1 + _dot((hid * hid).astype(BF16), wdown_ref[...])
    out_ref[...] = _rms(h2, fw_ref[...])


def _mlp(x2d, att, ssm, wo, n2, wup, wdown, fw, tm):
    rows = x2d.shape[0]
    tile = lambda w: pl.BlockSpec((tm, w), lambda i: (i, 0))
    return pl.pallas_call(
        _mlp_body, out_shape=jax.ShapeDtypeStruct((rows, D_MODEL), F32), grid=(rows // tm,),
        in_specs=[tile(D_MODEL), tile(ATT_VDIM), tile(D_SSM), _const_spec(wo.shape),
                  _const_spec((1, D_MODEL)), _const_spec(wup.shape), _const_spec(wdown.shape),
                  _const_spec((1, D_MODEL))],
        out_specs=tile(D_MODEL),
        compiler_params=_params(("arbitrary",)), name="mlp",
    )(x2d, att, ssm, wo, n2, wup, wdown, fw)


def _head_lane_vec(fwd_vals, bwd_vals):
    v = jnp.zeros((1, LANES), F32)
    v = v.at[0, 0:H_SSM].set(fwd_vals.astype(F32))
    return v.at[0, H_SSM:2 * H_SSM].set(bwd_vals.astype(F32))


def _expand_matrix(lane_lo):
    lane = jnp.arange(LANES)[:, None]
    ch = jnp.arange(D_SSM)[None, :] // SSM_HD
    return (lane == ch + lane_lo).astype(BF16)


def _trunk(x, meta_parts, p):
    nb, seq, _ = x.shape
    x2d = x.reshape(nb * seq, D_MODEL)
    k_meta, vT_meta, xbc_meta, dt_meta = meta_parts
    tm = min(ROW_TILE, seq)
    qT, k, vT, z, xbc_raw, dt_raw = _inproj(x2d, p["n1"], *p["w_in"], tm)
    act, meta_act = _conv(xbc_raw, xbc_meta, p["conv_w8"], p["conv_b"], nb, seq, tm)
    ssm = _ssd(act, meta_act, dt_raw, dt_meta, z, p["ssd_consts"], nb, seq)
    att = _attention(qT, k, vT, k_meta, vT_meta, p["slopes"], p["lq1"], p["lk1"], p["lq2"],
                     p["lk2"], p["attn_nw"], nb, seq)
    y = _mlp(x2d, att, ssm, p["wo"], p["n2"], p["wup"], p["wdown"], p["fw"], tm)
    return y.reshape(nb, seq, D_MODEL)


@jax.jit
def kernel(x_prompt, x_sample, meta_tokens, norm1_w, w_in, conv_w, conv_b, lambda_q1, lambda_k1,
           lambda_q2, lambda_k2, attn_norm_w, dt_bias_f, dt_bias_b, a_log_f, a_log_b, d_skip,
           ssm_norm_w, w_out, norm2_w, w_up, w_down, final_norm_w):
    li = 0
    w = w_in[li].astype(BF16)
    cuts = (0, ATT_VDIM, 2 * ATT_VDIM, 3 * ATT_VDIM, 3 * ATT_VDIM + D_SSM,
            3 * ATT_VDIM + D_SSM + CONV_DIM)
    wdt = jnp.zeros((D_MODEL, LANES), BF16).at[:, 0:2 * H_SSM].set(w[:, cuts[5]:])
    t = jnp.arange(CHUNK)
    p = {
        "n1": norm1_w[li].reshape(1, D_MODEL).astype(F32),
        "w_in": tuple(w[:, cuts[j]:cuts[j + 1]] for j in range(5)) + (wdt,),
        "conv_w8": jnp.zeros((HALO, CONV_DIM), F32).at[0:D_CONV].set(conv_w[li].astype(F32)),
        "conv_b": conv_b[li].reshape(1, CONV_DIM).astype(F32),
        "ssd_consts": (
            (t[:, None] >= t[None, :]).astype(BF16),
            (t[None, :] >= t[:, None]).astype(BF16),
            _expand_matrix(0), _expand_matrix(H_SSM),
            _head_lane_vec(dt_bias_f[li], dt_bias_b[li]),
            _head_lane_vec(a_log_f[li], a_log_b[li]),
            jnp.repeat(d_skip[li].astype(F32), SSM_HD).reshape(1, D_SSM),
            ssm_norm_w[li].reshape(1, D_SSM).astype(F32),
        ),
        "slopes": jnp.broadcast_to(
            (2.0 ** (-8.0 * (jnp.arange(N_HEADS, dtype=F32) + 1.0) / N_HEADS))[:, None, None],
            (N_HEADS, 1, LANES)),
        "lq1": lambda_q1[li].reshape(1, ATT_DH).astype(F32),
        "lk1": lambda_k1[li].reshape(1, ATT_DH).astype(F32),
        "lq2": lambda_q2[li].reshape(1, ATT_DH).astype(F32),
        "lk2": lambda_k2[li].reshape(1, ATT_DH).astype(F32),
        "attn_nw": attn_norm_w[li].reshape(1, HEAD_W).astype(F32),
        "wo": w_out[li].astype(BF16),
        "n2": norm2_w[li].reshape(1, D_MODEL).astype(F32),
        "wup": w_up[li].astype(BF16),
        "wdown": w_down[li].astype(BF16),
        "fw": final_norm_w.reshape(1, D_MODEL).astype(F32),
    }
    meta_block = jnp.concatenate(
        [jnp.zeros((NPAD, D_MODEL), F32), meta_tokens.astype(F32)], axis=0)
    _, k_m, vT_m, _, xbc_m, dt_m = _inproj(meta_block, p["n1"], *p["w_in"], CHUNK)
    meta_parts = (k_m, vT_m, xbc_m, dt_m)
    return (_trunk(x_prompt, meta_parts, p), _trunk(x_sample, meta_parts, p))
```

```python
import functools
import math

import jax
import jax.numpy as jnp
from jax import lax
from jax.experimental import pallas as pl
from jax.experimental.pallas import tpu as pltpu

F32 = jnp.float32
BF16 = jnp.bfloat16

D_MODEL = 1024
N_META = 16
CHUNK = 128
NPAD = CHUNK - N_META
N_HEADS = 8
ATT_DH = 64
HEAD_W = 2 * ATT_DH
ATT_VDIM = N_HEADS * HEAD_W
D_SSM = 1024
H_SSM = 16
SSM_HD = 64
SSM_G = 2
SSM_N = 128
D_CONV = 5
CONV_DIM = D_SSM + 2 * SSM_G * SSM_N
D_FF = 4 * D_MODEL
EPS = 1e-5
LAM_INIT = 0.8 - 0.6 * math.exp(-0.3 * 0)
NEG = -1e30

LANES = 128
HALO = 8
VMEM_LIMIT = 56 * 1024 * 1024

ROW_TILE = 512
ONES_ROWS = 16


def _dot(a, b):
    return jnp.dot(a, b, preferred_element_type=F32)


def _rms(x, w):
    return x * lax.rsqrt(jnp.mean(x * x, axis=-1, keepdims=True) + EPS) * w


def _silu(x):
    return x / (1.0 + jnp.exp(-x))


def _const_spec(shape):
    nd = len(shape)
    return pl.BlockSpec(shape, lambda *_: (0,) * nd)


def _params(sem):
    return pltpu.CompilerParams(dimension_semantics=sem, vmem_limit_bytes=VMEM_LIMIT)


def _inproj_body(x_ref, n1_ref, wq_ref, wk_ref, wv_ref, wz_ref, wx_ref, wdt_ref,
                 qT_ref, k_ref, vT_ref, z_ref, xbc_ref, dt_ref):
    u = _rms(x_ref[...], n1_ref[...]).astype(BF16)
    q = _dot(u, wq_ref[...]) * (ATT_DH ** -0.5)
    for h in range(N_HEADS):
        qT_ref[h] = q[:, h * HEAD_W:(h + 1) * HEAD_W].T.astype(BF16)
    k = _dot(u, wk_ref[...])
    for h in range(N_HEADS):
        k_ref[h] = k[:, h * HEAD_W:(h + 1) * HEAD_W].astype(BF16)
    v = _dot(u, wv_ref[...])
    for h in range(N_HEADS):
        vT_ref[h, 0] = v[:, h * HEAD_W:(h + 1) * HEAD_W].T.astype(BF16)
    z_ref[...] = _dot(u, wz_ref[...])
    xbc_ref[...] = _dot(u, wx_ref[...])
    dt_ref[...] = _dot(u, wdt_ref[...])


def _inproj(x2d, n1, wq, wk, wv, wz, wx, wdt, tm):
    rows = x2d.shape[0]
    grid = (rows // tm,)
    out_shape = (
        jax.ShapeDtypeStruct((N_HEADS, HEAD_W, rows), BF16),
        jax.ShapeDtypeStruct((N_HEADS, rows, HEAD_W), BF16),
        jax.ShapeDtypeStruct((N_HEADS, rows // tm, HEAD_W, tm), BF16),
        jax.ShapeDtypeStruct((rows, D_SSM), F32),
        jax.ShapeDtypeStruct((rows, CONV_DIM), F32),
        jax.ShapeDtypeStruct((rows, LANES), F32),
    )
    in_specs = [
        pl.BlockSpec((tm, D_MODEL), lambda i: (i, 0)),
        _const_spec((1, D_MODEL)),
        _const_spec(wq.shape), _const_spec(wk.shape), _const_spec(wv.shape),
        _const_spec(wz.shape), _const_spec(wx.shape), _const_spec(wdt.shape),
    ]
    out_specs = (
        pl.BlockSpec((N_HEADS, HEAD_W, tm), lambda i: (0, 0, i)),
        pl.BlockSpec((N_HEADS, tm, HEAD_W), lambda i: (0, i, 0)),
        pl.BlockSpec((N_HEADS, 1, HEAD_W, tm), lambda i: (0, i, 0, 0)),
        pl.BlockSpec((tm, D_SSM), lambda i: (i, 0)),
        pl.BlockSpec((tm, CONV_DIM), lambda i: (i, 0)),
        pl.BlockSpec((tm, LANES), lambda i: (i, 0)),
    )
    return pl.pallas_call(
        _inproj_body, out_shape=out_shape, grid=grid, in_specs=in_specs, out_specs=out_specs,
        compiler_params=_params(("arbitrary",)), name="inproj",
    )(x2d, n1, wq, wk, wv, wz, wx, wdt)


def _conv_taps(ext, w, b, rows):
    acc = b + w[0:1, :] * ext[HALO - 2:HALO - 2 + rows, :]
    for j in range(1, D_CONV):
        acc = acc + w[j:j + 1, :] * ext[HALO - 2 + j:HALO - 2 + j + rows, :]
    return _silu(acc)


def _conv_body(main_ref, prev_ref, next_ref, metaraw_ref, w_ref, b_ref, act_ref, metaact_ref,
               *, tc, nt):
    i = pl.program_id(1)
    main = main_ref[...]
    w = w_ref[...]
    b = b_ref[...]
    prev = jnp.where(i == 0, metaraw_ref[CHUNK - HALO:CHUNK, :], prev_ref[...])
    nxt = jnp.where(i == nt - 1, 0.0, next_ref[...])
    ext = jnp.concatenate([prev, main, nxt], axis=0)
    act_ref[...] = _conv_taps(ext, w, b, tc)

    @pl.when(i == 0)
    def _():
        zeros = jnp.zeros((HALO, CONV_DIM), F32)
        extm = jnp.concatenate([zeros, metaraw_ref[...], main[0:HALO, :]], axis=0)
        am = _conv_taps(extm, w, b, CHUNK)
        row = lax.broadcasted_iota(jnp.int32, (CHUNK, 1), 0)
        metaact_ref[0] = jnp.where(row >= NPAD, am, 0.0)


def _conv(xbc_raw, meta_raw, conv_w8, conv_b, nb, seq, tc):
    nt = seq // tc
    hb = tc // HALO
    nblk8 = (nb * seq) // HALO
    in_specs = [
        pl.BlockSpec((tc, CONV_DIM), lambda b, i: (b * nt + i, 0)),
        pl.BlockSpec((HALO, CONV_DIM), lambda b, i: (jnp.maximum((b * nt + i) * hb - 1, 0), 0)),
        pl.BlockSpec((HALO, CONV_DIM),
                     lambda b, i: (jnp.minimum((b * nt + i + 1) * hb, nblk8 - 1), 0)),
        _const_spec((CHUNK, CONV_DIM)),
        _const_spec((HALO, CONV_DIM)),
        _const_spec((1, CONV_DIM)),
    ]
    out_specs = (
        pl.BlockSpec((tc, CONV_DIM), lambda b, i: (b * nt + i, 0)),
        pl.BlockSpec((1, CHUNK, CONV_DIM), lambda b, i: (b, 0, 0)),
    )
    out_shape = (
        jax.ShapeDtypeStruct((nb * seq, CONV_DIM), F32),
        jax.ShapeDtypeStruct((nb, CHUNK, CONV_DIM), F32),
    )
    return pl.pallas_call(
        functools.partial(_conv_body, tc=tc, nt=nt), out_shape=out_shape, grid=(nb, nt),
        in_specs=in_specs, out_specs=out_specs,
        compiler_params=_params(("arbitrary", "arbitrary")), name="conv",
    )(xbc_raw, xbc_raw, xbc_raw, meta_raw, conv_w8, conv_b)


def _split3(x):
    hi = x.astype(BF16)
    r = x - hi.astype(F32)
    mid = r.astype(BF16)
    lo = (r - mid.astype(F32)).astype(BF16)
    return hi, mid, lo


def _expand_heads(x, e):
    hi = x.astype(BF16)
    lo = (x - hi.astype(F32)).astype(BF16)
    return _dot(hi, e) + _dot(lo, e)


def _ssd_chunk(act, dtr, valid, tri, e, biasv, alogv, lane_lo, st_ref, fwd):
    xs = act[:, :D_SSM]
    lane = lax.broadcasted_iota(jnp.int32, (1, LANES), 1)
    hmask = (lane >= lane_lo) & (lane < lane_lo + H_SSM)
    x = dtr + biasv
    dt = jnp.maximum(x, 0.0) + jnp.log(1.0 + jnp.exp(-jnp.abs(x)))
    dt = jnp.where(hmask & valid, dt, 0.0)
    adt = dt * (-jnp.exp(alogv))
    hi, mid, lo = _split3(adt)
    acs = _dot(tri, hi) + _dot(tri, mid) + _dot(tri, lo)
    acs_t = acs.T
    edge = acs[CHUNK - 1:CHUNK, :] if fwd else acs[0:1, :]
    dt_x = _expand_heads(dt, e)
    ein_x = _expand_heads(jnp.exp(acs), e)
    est_x = _expand_heads(jnp.exp(edge - acs), e)
    cd_x = ein_x[CHUNK - 1:CHUNK, :] if fwd else ein_x[0:1, :]
    xg = xs * dt_x
    xgd = (xg * est_x).astype(BF16)
    li = lax.broadcasted_iota(jnp.int32, (CHUNK, CHUNK), 0)
    si = lax.broadcasted_iota(jnp.int32, (CHUNK, CHUNK), 1)
    tmask = (li >= si) if fwd else (si >= li)
    left = lax.broadcasted_iota(jnp.int32, (1, LANES), 1) < SSM_HD
    ys = []
    heads_per_group = H_SSM // SSM_G
    for g in range(SSM_G):
        bg = act[:, D_SSM + g * SSM_N:D_SSM + (g + 1) * SSM_N]
        cg = act[:, D_SSM + SSM_G * SSM_N + g * SSM_N:D_SSM + SSM_G * SSM_N + (g + 1) * SSM_N]
        cb = cg.astype(BF16)
        bb = bg.astype(BF16)
        gm = lax.dot_general(cb, bb, (((1,), (1,)), ((), ())), preferred_element_type=F32)
        st = st_ref[g]
        yoff = _dot(cb, st.astype(BF16))
        for jp in range(heads_per_group // 2):
            pair = g * (heads_per_group // 2) + jp
            sl = slice(pair * LANES, (pair + 1) * LANES)
            xg_pair = xg[:, sl]
            ypair = ein_x[:, sl] * yoff[:, jp * LANES:(jp + 1) * LANES]
            for hh in range(2):
                c = lane_lo + pair * 2 + hh
                seg = acs[:, c:c + 1] - acs_t[c:c + 1, :]
                decay = jnp.exp(jnp.where(tmask, seg, NEG))
                md = (gm * decay).astype(BF16)
                half = left if hh == 0 else jnp.logical_not(left)
                ypair = ypair + _dot(md, jnp.where(half, xg_pair, 0.0).astype(BF16))
            ys.append(ypair)
        gsl = slice(g * heads_per_group * SSM_HD, (g + 1) * heads_per_group * SSM_HD)
        st_ref[g] = st * cd_x[:, gsl] + _dot(bg.T.astype(BF16), xgd[:, gsl])
    return jnp.concatenate(ys, axis=1), xs


def _ssd_fwd_body(actx_ref, actm_ref, dtx_ref, dtm_ref, tri_ref, e_ref, biasv_ref, alogv_ref,
                  yf_ref, st_ref):
    c = pl.program_id(1)

    @pl.when(c == 0)
    def _():
        st_ref[...] = jnp.zeros_like(st_ref)

    is_meta = c == 0
    act = jnp.where(is_meta, actm_ref[0], actx_ref[...])
    dtr = jnp.where(is_meta, dtm_ref[...], dtx_ref[...])
    row = lax.broadcasted_iota(jnp.int32, (CHUNK, 1), 0)
    valid = jnp.logical_or(jnp.logical_not(is_meta), row >= NPAD)
    y, _ = _ssd_chunk(act, dtr, valid, tri_ref[...], e_ref[...], biasv_ref[...], alogv_ref[...],
                      0, st_ref, True)
    yf_ref[...] = y


def _ssd_bwd_body(act_ref, dt_ref, yf_ref, z_ref, tri_ref, e_ref, biasv_ref, alogv_ref,
                  dskip_ref, nw_ref, out_ref, st_ref):
    c = pl.program_id(1)

    @pl.when(c == 0)
    def _():
        st_ref[...] = jnp.zeros_like(st_ref)

    yb, xs = _ssd_chunk(act_ref[...], dt_ref[...], True, tri_ref[...], e_ref[...],
                        biasv_ref[...], alogv_ref[...], H_SSM, st_ref, False)
    y = yf_ref[...] + yb + xs * dskip_ref[...]
    out_ref[...] = _rms(y * _silu(z_ref[...]), nw_ref[...]).astype(BF16)


def _ssd(act, meta_act, dt_raw, dt_meta, z, consts, nb, seq):
    nc = seq // CHUNK
    tril, triu, e_f, e_b, biasv, alogv, dskip_x, ssm_nw = consts
    st_shape = pltpu.VMEM((SSM_G, SSM_N, (H_SSM // SSM_G) * SSM_HD), F32)

    xblk = lambda b, c: (b * nc + jnp.maximum(c - 1, 0), 0)
    y_f = pl.pallas_call(
        _ssd_fwd_body,
        out_shape=jax.ShapeDtypeStruct((nb * seq, D_SSM), F32),
        grid=(nb, nc + 1),
        in_specs=[
            pl.BlockSpec((CHUNK, CONV_DIM), xblk),
            pl.BlockSpec((1, CHUNK, CONV_DIM), lambda b, c: (b, 0, 0)),
            pl.BlockSpec((CHUNK, LANES), xblk),
            _const_spec((CHUNK, LANES)),
            _const_spec((CHUNK, CHUNK)), _const_spec((LANES, D_SSM)),
            _const_spec((1, LANES)), _const_spec((1, LANES)),
        ],
        out_specs=pl.BlockSpec((CHUNK, D_SSM), xblk),
        scratch_shapes=[st_shape],
        compiler_params=_params(("arbitrary", "arbitrary")), name="ssd_fwd",
    )(act, meta_act, dt_raw, dt_meta, tril, e_f, biasv, alogv)

    rblk = lambda b, c: (b * nc + nc - 1 - c, 0)
    return pl.pallas_call(
        _ssd_bwd_body,
        out_shape=jax.ShapeDtypeStruct((nb * seq, D_SSM), BF16),
        grid=(nb, nc),
        in_specs=[
            pl.BlockSpec((CHUNK, CONV_DIM), rblk),
            pl.BlockSpec((CHUNK, LANES), rblk),
            pl.BlockSpec((CHUNK, D_SSM), rblk),
            pl.BlockSpec((CHUNK, D_SSM), rblk),
            _const_spec((CHUNK, CHUNK)), _const_spec((LANES, D_SSM)),
            _const_spec((1, LANES)), _const_spec((1, LANES)),
            _const_spec((1, D_SSM)), _const_spec((1, D_SSM)),
        ],
        out_specs=pl.BlockSpec((CHUNK, D_SSM), rblk),
        scratch_shapes=[st_shape],
        compiler_params=_params(("arbitrary", "arbitrary")), name="ssd_bwd",
    )(act, dt_raw, y_f, z, triu, e_b, biasv, alogv, dskip_x, ssm_nw)


def _attn_body(qT_ref, k_ref, vT_ref, km_ref, vmT_ref, slope_ref, lq1_ref, lk1_ref, lq2_ref,
               lk2_ref, nw_ref, out_ref, qaug_ref, sa_ref, sb_ref, acc_ref, *, t, seq):
    i = pl.program_id(2)
    nkv = seq // t
    slope = slope_ref[0]
    slope_q = slope[:, 0:1]
    qT = qT_ref[0]
    row = lax.broadcasted_iota(jnp.int32, (HEAD_W, 1), 0)
    zero = jnp.zeros_like(qT)
    qz = (jnp.where(row < ATT_DH, qT, zero), jnp.where(row >= ATT_DH, qT, zero))
    frow = lax.broadcasted_iota(jnp.int32, (HEAD_W, t), 0)
    fcol = lax.broadcasted_iota(jnp.int32, (HEAD_W, t), 1)
    c_hi = (fcol // 256) * 256
    feat = jnp.where(frow < 2, 1.0,
                     jnp.where(frow == 2, -slope_q * c_hi.astype(F32),
                               jnp.where(frow == 3, -slope_q * (fcol - c_hi).astype(F32), 0.0)))
    for sg, f in enumerate((feat, -feat)):
        fb = f.astype(BF16)
        for mp in range(2):
            qaug_ref[sg, mp] = jnp.concatenate([qz[mp], fb], axis=0)

    def key_feat(rows, rel, valid):
        lane = lax.broadcasted_iota(jnp.int32, (rows, LANES), 1)
        r_hi = (rel // 256) * 256
        f = jnp.where(lane == 0, slope * r_hi.astype(F32),
                      jnp.where(lane == 1, slope * (rel - r_hi).astype(F32),
                                jnp.where(lane < 4, 1.0, 0.0)))
        if valid is not None:
            f = jnp.where(jnp.logical_and(lane == 1, jnp.logical_not(valid)), NEG, f)
        return f.astype(BF16)

    kfeat = key_feat(t, lax.broadcasted_iota(jnp.int32, (t, LANES), 0), None)
    ones_rows = jnp.ones((ONES_ROWS, t), BF16)

    def block_of(n):
        after = (n >= i).astype(jnp.int32)
        j = n + after
        cterm = slope_q * (-(jnp.abs(j - i) * t).astype(F32))
        return j, after, cterm

    def produce(n, s_ref):
        j, sg, _ = block_of(n)
        k0 = pl.multiple_of(j * t, t)
        lhs = jnp.concatenate([k_ref[0, pl.ds(k0, t), :], kfeat], axis=1)
        mbs = []
        for mp in range(2):
            s = _dot(lhs, qaug_ref[sg, mp])
            s_ref[mp] = s
            mbs.append(jnp.max(s, axis=0, keepdims=True))
        return tuple(mbs)

    def softmax_pv(s, mb, cterm, m_old, mp, v_aug):
        m_new = jnp.maximum(m_old, mb + cterm)
        p = jnp.exp(s - (m_new - cterm)).astype(BF16)
        alpha = jnp.exp(m_old - m_new)
        acc_ref[mp] = acc_ref[mp] * alpha + _dot(v_aug, p)
        return m_new

    def consume(n, s_ref, mbs, ms):
        j, _, cterm = block_of(n)
        v_aug = jnp.concatenate([vT_ref[0, j], ones_rows], axis=0)
        return tuple(softmax_pv(s_ref[mp], mbs[mp], cterm, ms[mp], mp, v_aug) for mp in range(2))

    acc_ref[...] = jnp.zeros_like(acc_ref)
    ms = (jnp.full((1, t), NEG, F32),) * 2
    n_off = nkv - 1
    if n_off >= 1:
        mbs = produce(0, sa_ref)
        pairs, rem = divmod(n_off - 1, 2)

        def pair_body(u, carry):
            ms, mbs = carry[:2], carry[2:]
            n = 2 * u
            ms = consume(n, sa_ref, mbs, ms)
            mbs = produce(n + 1, sb_ref)
            ms = consume(n + 1, sb_ref, mbs, ms)
            mbs = produce(n + 2, sa_ref)
            return ms + mbs

        carry = lax.fori_loop(0, pairs, pair_body, ms + mbs, unroll=3 if pairs % 3 == 0 else 1)
        ms, mbs = carry[:2], carry[2:]
        last_ref = sa_ref
        if rem:
            ms = consume(2 * pairs, sa_ref, mbs, ms)
            mbs = produce(2 * pairs + 1, sb_ref)
            last_ref = sb_ref
        ms = consume(n_off - 1, last_ref, mbs, ms)

    mrow = lax.broadcasted_iota(jnp.int32, (CHUNK, LANES), 0)
    lhs_m = jnp.concatenate([km_ref[0], key_feat(CHUNK, mrow - CHUNK, mrow >= NPAD)], axis=1)
    vm_aug = jnp.concatenate([vmT_ref[0, 0], jnp.ones((ONES_ROWS, CHUNK), BF16)], axis=0)
    cterm_m = slope_q * (-(i * t).astype(F32))
    kd0 = pl.multiple_of(i * t, t)
    lhs_d = jnp.concatenate([k_ref[0, pl.ds(kd0, t), :], kfeat], axis=1)
    v_d = jnp.concatenate([vT_ref[0, i], ones_rows], axis=0)
    for mp in range(2):
        s = _dot(lhs_m, qaug_ref[0, mp])
        m1 = softmax_pv(s, jnp.max(s, axis=0, keepdims=True), cterm_m, ms[mp], mp, vm_aug)
        s = jnp.minimum(_dot(lhs_d, qaug_ref[0, mp]), _dot(lhs_d, qaug_ref[1, mp]))
        softmax_pv(s, jnp.max(s, axis=0, keepdims=True), 0.0, m1, mp, v_d)

    lam = (jnp.exp(jnp.sum(lq1_ref[...] * lk1_ref[...], axis=-1, keepdims=True))
           - jnp.exp(jnp.sum(lq2_ref[...] * lk2_ref[...], axis=-1, keepdims=True)) + LAM_INIT)
    o0 = acc_ref[0, 0:HEAD_W, :] / acc_ref[0, HEAD_W:HEAD_W + 1, :]
    o1 = acc_ref[1, 0:HEAD_W, :] / acc_ref[1, HEAD_W:HEAD_W + 1, :]
    o = o0 - lam * o1
    on = o * lax.rsqrt(jnp.mean(o * o, axis=0, keepdims=True) + EPS)
    out_ref[...] = ((on.T * nw_ref[...]) * (1.0 - LAM_INIT)).astype(BF16)


def _attention(qT, k, vT, k_meta, vT_meta, slopes, lq1, lk1, lq2, lk2, attn_nw, nb, seq):
    t = min(ROW_TILE, seq)
    nq = seq // t
    grid = (N_HEADS, nb, nq)
    vec = lambda n: _const_spec((1, n))
    in_specs = [
        pl.BlockSpec((1, HEAD_W, t), lambda h, b, i: (h, 0, b * nq + i)),
        pl.BlockSpec((1, seq, HEAD_W), lambda h, b, i: (h, b, 0)),
        pl.BlockSpec((1, nq, HEAD_W, t), lambda h, b, i: (h, b, 0, 0)),
        pl.BlockSpec((1, CHUNK, HEAD_W), lambda h, b, i: (h, 0, 0)),
        pl.BlockSpec((1, 1, HEAD_W, CHUNK), lambda h, b, i: (h, 0, 0, 0)),
        pl.BlockSpec((1, 1, LANES), lambda h, b, i: (h, 0, 0)),
        vec(ATT_DH), vec(ATT_DH), vec(ATT_DH), vec(ATT_DH), vec(HEAD_W),
    ]
    return pl.pallas_call(
        functools.partial(_attn_body, t=t, seq=seq),
        out_shape=jax.ShapeDtypeStruct((nb * seq, ATT_VDIM), BF16),
        grid=grid, in_specs=in_specs,
        out_specs=pl.BlockSpec((t, HEAD_W), lambda h, b, i: (b * nq + i, h)),
        scratch_shapes=[pltpu.VMEM((2, 2, 2 * HEAD_W, t), BF16),
                        pltpu.VMEM((2, t, t), F32), pltpu.VMEM((2, t, t), F32),
                        pltpu.VMEM((2, HEAD_W + ONES_ROWS, t), F32)],
        compiler_params=_params(("arbitrary", "arbitrary", "arbitrary")), name="attn",
    )(qT, k, vT, k_meta, vT_meta, slopes, lq1, lk1, lq2, lk2, attn_nw)


def _mlp_body(x_ref, att_ref, ssm_ref, wo_ref, n2_ref, wup_ref, wdown_ref, fw_ref, out_ref):
    h1 = (x_ref[...] + _dot(att_ref[...], wo_ref[0:ATT_VDIM, :])
          + _dot(ssm_ref[...], wo_ref[ATT_VDIM:ATT_VDIM + D_SSM, :]))
    u2 = _rms(h1, n2_ref[...]).astype(BF16)
    hid = jnp.maximum(_dot(u2, wup_ref[...]), 0.0)
    h2 = h1 + _dot((hid * hid).astype(BF16), wdown_ref[...])
    out_ref[...] = _rms(h2, fw_ref[...])


def _mlp(x2d, att, ssm, wo, n2, wup, wdown, fw, tm):
    rows = x2d.shape[0]
    tile = lambda w: pl.BlockSpec((tm, w), lambda i: (i, 0))
    return pl.pallas_call(
        _mlp_body, out_shape=jax.ShapeDtypeStruct((rows, D_MODEL), F32), grid=(rows // tm,),
        in_specs=[tile(D_MODEL), tile(ATT_VDIM), tile(D_SSM), _const_spec(wo.shape),
                  _const_spec((1, D_MODEL)), _const_spec(wup.shape), _const_spec(wdown.shape),
                  _const_spec((1, D_MODEL))],
        out_specs=tile(D_MODEL),
        compiler_params=_params(("arbitrary",)), name="mlp",
    )(x2d, att, ssm, wo, n2, wup, wdown, fw)


def _head_lane_vec(fwd_vals, bwd_vals):
    v = jnp.zeros((1, LANES), F32)
    v = v.at[0, 0:H_SSM].set(fwd_vals.astype(F32))
    return v.at[0, H_SSM:2 * H_SSM].set(bwd_vals.astype(F32))


def _expand_matrix(lane_lo):
    lane = jnp.arange(LANES)[:, None]
    ch = jnp.arange(D_SSM)[None, :] // SSM_HD
    return (lane == ch + lane_lo).astype(BF16)


def _trunk(x, meta_parts, p):
    nb, seq, _ = x.shape
    x2d = x.reshape(nb * seq, D_MODEL)
    k_meta, vT_meta, xbc_meta, dt_meta = meta_parts
    tm = min(ROW_TILE, seq)
    qT, k, vT, z, xbc_raw, dt_raw = _inproj(x2d, p["n1"], *p["w_in"], tm)
    act, meta_act = _conv(xbc_raw, xbc_meta, p["conv_w8"], p["conv_b"], nb, seq, tm)
    ssm = _ssd(act, meta_act, dt_raw, dt_meta, z, p["ssd_consts"], nb, seq)
    att = _attention(qT, k, vT, k_meta, vT_meta, p["slopes"], p["lq1"], p["lk1"], p["lq2"],
                     p["lk2"], p["attn_nw"], nb, seq)
    y = _mlp(x2d, att, ssm, p["wo"], p["n2"], p["wup"], p["wdown"], p["fw"], tm)
    return y.reshape(nb, seq, D_MODEL)


@jax.jit
def kernel(x_prompt, x_sample, meta_tokens, norm1_w, w_in, conv_w, conv_b, lambda_q1, lambda_k1,
           lambda_q2, lambda_k2, attn_norm_w, dt_bias_f, dt_bias_b, a_log_f, a_log_b, d_skip,
           ssm_norm_w, w_out, norm2_w, w_up, w_down, final_norm_w):
    li = 0
    w = w_in[li].astype(BF16)
    cuts = (0, ATT_VDIM, 2 * ATT_VDIM, 3 * ATT_VDIM, 3 * ATT_VDIM + D_SSM,
            3 * ATT_VDIM + D_SSM + CONV_DIM)
    wdt = jnp.zeros((D_MODEL, LANES), BF16).at[:, 0:2 * H_SSM].set(w[:, cuts[5]:])
    t = jnp.arange(CHUNK)
    p = {
        "n1": norm1_w[li].reshape(1, D_MODEL).astype(F32),
        "w_in": tuple(w[:, cuts[j]:cuts[j + 1]] for j in range(5)) + (wdt,),
        "conv_w8": jnp.zeros((HALO, CONV_DIM), F32).at[0:D_CONV].set(conv_w[li].astype(F32)),
        "conv_b": conv_b[li].reshape(1, CONV_DIM).astype(F32),
        "ssd_consts": (
            (t[:, None] >= t[None, :]).astype(BF16),
            (t[None, :] >= t[:, None]).astype(BF16),
            _expand_matrix(0), _expand_matrix(H_SSM),
            _head_lane_vec(dt_bias_f[li], dt_bias_b[li]),
            _head_lane_vec(a_log_f[li], a_log_b[li]),
            jnp.repeat(d_skip[li].astype(F32), SSM_HD).reshape(1, D_SSM),
            ssm_norm_w[li].reshape(1, D_SSM).astype(F32),
        ),
        "slopes": jnp.broadcast_to(
            (2.0 ** (-8.0 * (jnp.arange(N_HEADS, dtype=F32) + 1.0) / N_HEADS))[:, None, None],
            (N_HEADS, 1, LANES)),
        "lq1": lambda_q1[li].reshape(1, ATT_DH).astype(F32),
        "lk1": lambda_k1[li].reshape(1, ATT_DH).astype(F32),
        "lq2": lambda_q2[li].reshape(1, ATT_DH).astype(F32),
        "lk2": lambda_k2[li].reshape(1, ATT_DH).astype(F32),
        "attn_nw": attn_norm_w[li].reshape(1, HEAD_W).astype(F32),
        "wo": w_out[li].astype(BF16),
        "n2": norm2_w[li].reshape(1, D_MODEL).astype(F32),
        "wup": w_up[li].astype(BF16),
        "wdown": w_down[li].astype(BF16),
        "fw": final_norm_w.reshape(1, D_MODEL).astype(F32),
    }
    meta_block = jnp.concatenate(
        [jnp.zeros((NPAD, D_MODEL), F32), meta_tokens.astype(F32)], axis=0)
    _, k_m, vT_m, _, xbc_m, dt_m = _inproj(meta_block, p["n1"], *p["w_in"], CHUNK)
    meta_parts = (k_m, vT_m, xbc_m, dt_m)
    return (_trunk(x_prompt, meta_parts, p), _trunk(x_sample, meta_parts, p))
```

```python
import functools
import math

import jax
import jax.numpy as jnp
from jax import lax
from jax.experimental import pallas as pl
from jax.experimental.pallas import tpu as pltpu

F32 = jnp.float32
BF16 = jnp.bfloat16

D_MODEL = 1024
N_META = 16
CHUNK = 128
NPAD = CHUNK - N_META
N_HEADS = 8
ATT_DH = 64
HEAD_W = 2 * ATT_DH
ATT_VDIM = N_HEADS * HEAD_W
D_SSM = 1024
H_SSM = 16
SSM_HD = 64
SSM_G = 2
SSM_N = 128
D_CONV = 5
CONV_DIM = D_SSM + 2 * SSM_G * SSM_N
D_FF = 4 * D_MODEL
EPS = 1e-5
LAM_INIT = 0.8 - 0.6 * math.exp(-0.3 * 0)
NEG = -1e30

LANES = 128
HALO = 8
VMEM_LIMIT = 56 * 1024 * 1024

ROW_TILE = 512
ONES_ROWS = 16
ATTN_UNROLL_MAX_UNITS = 24


def _dot(a, b):
    return jnp.dot(a, b, preferred_element_type=F32)


def _rms(x, w):
    return x * lax.rsqrt(jnp.mean(x * x, axis=-1, keepdims=True) + EPS) * w


def _silu(x):
    return x / (1.0 + jnp.exp(-x))


def _const_spec(shape):
    nd = len(shape)
    return pl.BlockSpec(shape, lambda *_: (0,) * nd)


def _params(sem):
    return pltpu.CompilerParams(dimension_semantics=sem, vmem_limit_bytes=VMEM_LIMIT)


def _inproj_body(x_ref, n1_ref, wq_ref, wk_ref, wv_ref, wz_ref, wx_ref, wdt_ref,
                 qT_ref, k_ref, vT_ref, z_ref, xbc_ref, dt_ref):
    u = _rms(x_ref[...], n1_ref[...]).astype(BF16)
    q = _dot(u, wq_ref[...]) * (ATT_DH ** -0.5)
    for h in range(N_HEADS):
        qT_ref[h] = q[:, h * HEAD_W:(h + 1) * HEAD_W].T.astype(BF16)
    k = _dot(u, wk_ref[...])
    for h in range(N_HEADS):
        k_ref[h] = k[:, h * HEAD_W:(h + 1) * HEAD_W].astype(BF16)
    v = _dot(u, wv_ref[...])
    for h in range(N_HEADS):
        vT_ref[h, 0] = v[:, h * HEAD_W:(h + 1) * HEAD_W].T.astype(BF16)
    z_ref[...] = _dot(u, wz_ref[...])
    xbc_ref[...] = _dot(u, wx_ref[...])
    dt_ref[...] = _dot(u, wdt_ref[...])


def _inproj(x2d, n1, wq, wk, wv, wz, wx, wdt, tm):
    rows = x2d.shape[0]
    grid = (rows // tm,)
    out_shape = (
        jax.ShapeDtypeStruct((N_HEADS, HEAD_W, rows), BF16),
        jax.ShapeDtypeStruct((N_HEADS, rows, HEAD_W), BF16),
        jax.ShapeDtypeStruct((N_HEADS, rows // tm, HEAD_W, tm), BF16),
        jax.ShapeDtypeStruct((rows, D_SSM), F32),
        jax.ShapeDtypeStruct((rows, CONV_DIM), F32),
        jax.ShapeDtypeStruct((rows, LANES), F32),
    )
    in_specs = [
        pl.BlockSpec((tm, D_MODEL), lambda i: (i, 0)),
        _const_spec((1, D_MODEL)),
        _const_spec(wq.shape), _const_spec(wk.shape), _const_spec(wv.shape),
        _const_spec(wz.shape), _const_spec(wx.shape), _const_spec(wdt.shape),
    ]
    out_specs = (
        pl.BlockSpec((N_HEADS, HEAD_W, tm), lambda i: (0, 0, i)),
        pl.BlockSpec((N_HEADS, tm, HEAD_W), lambda i: (0, i, 0)),
        pl.BlockSpec((N_HEADS, 1, HEAD_W, tm), lambda i: (0, i, 0, 0)),
        pl.BlockSpec((tm, D_SSM), lambda i: (i, 0)),
        pl.BlockSpec((tm, CONV_DIM), lambda i: (i, 0)),
        pl.BlockSpec((tm, LANES), lambda i: (i, 0)),
    )
    return pl.pallas_call(
        _inproj_body, out_shape=out_shape, grid=grid, in_specs=in_specs, out_specs=out_specs,
        compiler_params=_params(("arbitrary",)), name="inproj",
    )(x2d, n1, wq, wk, wv, wz, wx, wdt)


def _conv_taps(ext, w, b, rows):
    acc = b + w[0:1, :] * ext[HALO - 2:HALO - 2 + rows, :]
    for j in range(1, D_CONV):
        acc = acc + w[j:j + 1, :] * ext[HALO - 2 + j:HALO - 2 + j + rows, :]
    return _silu(acc)


def _conv_body(main_ref, prev_ref, next_ref, metaraw_ref, w_ref, b_ref, act_ref, metaact_ref,
               *, tc, nt):
    i = pl.program_id(1)
    main = main_ref[...]
    w = w_ref[...]
    b = b_ref[...]
    prev = jnp.where(i == 0, metaraw_ref[CHUNK - HALO:CHUNK, :], prev_ref[...])
    nxt = jnp.where(i == nt - 1, 0.0, next_ref[...])
    ext = jnp.concatenate([prev, main, nxt], axis=0)
    act_ref[...] = _conv_taps(ext, w, b, tc)

    @pl.when(i == 0)
    def _():
        zeros = jnp.zeros((HALO, CONV_DIM), F32)
        extm = jnp.concatenate([zeros, metaraw_ref[...], main[0:HALO, :]], axis=0)
        am = _conv_taps(extm, w, b, CHUNK)
        row = lax.broadcasted_iota(jnp.int32, (CHUNK, 1), 0)
        metaact_ref[0] = jnp.where(row >= NPAD, am, 0.0)


def _conv(xbc_raw, meta_raw, conv_w8, conv_b, nb, seq, tc):
    nt = seq // tc
    hb = tc // HALO
    nblk8 = (nb * seq) // HALO
    in_specs = [
        pl.BlockSpec((tc, CONV_DIM), lambda b, i: (b * nt + i, 0)),
        pl.BlockSpec((HALO, CONV_DIM), lambda b, i: (jnp.maximum((b * nt + i) * hb - 1, 0), 0)),
        pl.BlockSpec((HALO, CONV_DIM),
                     lambda b, i: (jnp.minimum((b * nt + i + 1) * hb, nblk8 - 1), 0)),
        _const_spec((CHUNK, CONV_DIM)),
        _const_spec((HALO, CONV_DIM)),
        _const_spec((1, CONV_DIM)),
    ]
    out_specs = (
        pl.BlockSpec((tc, CONV_DIM), lambda b, i: (b * nt + i, 0)),
        pl.BlockSpec((1, CHUNK, CONV_DIM), lambda b, i: (b, 0, 0)),
    )
    out_shape = (
        jax.ShapeDtypeStruct((nb * seq, CONV_DIM), F32),
        jax.ShapeDtypeStruct((nb, CHUNK, CONV_DIM), F32),
    )
    return pl.pallas_call(
        functools.partial(_conv_body, tc=tc, nt=nt), out_shape=out_shape, grid=(nb, nt),
        in_specs=in_specs, out_specs=out_specs,
        compiler_params=_params(("arbitrary", "arbitrary")), name="conv",
    )(xbc_raw, xbc_raw, xbc_raw, meta_raw, conv_w8, conv_b)


def _split3(x):
    hi = x.astype(BF16)
    r = x - hi.astype(F32)
    mid = r.astype(BF16)
    lo = (r - mid.astype(F32)).astype(BF16)
    return hi, mid, lo


def _expand_heads(x, e):
    hi = x.astype(BF16)
    lo = (x - hi.astype(F32)).astype(BF16)
    return _dot(hi, e) + _dot(lo, e)


def _ssd_chunk(act, dtr, valid, tri, e, biasv, alogv, lane_lo, st_ref, fwd):
    xs = act[:, :D_SSM]
    lane = lax.broadcasted_iota(jnp.int32, (1, LANES), 1)
    hmask = (lane >= lane_lo) & (lane < lane_lo + H_SSM)
    x = dtr + biasv
    dt = jnp.maximum(x, 0.0) + jnp.log(1.0 + jnp.exp(-jnp.abs(x)))
    dt = jnp.where(hmask & valid, dt, 0.0)
    adt = dt * (-jnp.exp(alogv))
    hi, mid, lo = _split3(adt)
    acs = _dot(tri, hi) + _dot(tri, mid) + _dot(tri, lo)
    acs_t = acs.T
    edge = acs[CHUNK - 1:CHUNK, :] if fwd else acs[0:1, :]
    dt_x = _expand_heads(dt, e)
    ein_x = _expand_heads(jnp.exp(acs), e)
    est_x = _expand_heads(jnp.exp(edge - acs), e)
    cd_x = ein_x[CHUNK - 1:CHUNK, :] if fwd else ein_x[0:1, :]
    xg = xs * dt_x
    xgd = (xg * est_x).astype(BF16)
    li = lax.broadcasted_iota(jnp.int32, (CHUNK, CHUNK), 0)
    si = lax.broadcasted_iota(jnp.int32, (CHUNK, CHUNK), 1)
    tmask = (li >= si) if fwd else (si >= li)
    left = lax.broadcasted_iota(jnp.int32, (1, LANES), 1) < SSM_HD
    ys = []
    heads_per_group = H_SSM // SSM_G
    for g in range(SSM_G):
        bg = act[:, D_SSM + g * SSM_N:D_SSM + (g + 1) * SSM_N]
        cg = act[:, D_SSM + SSM_G * SSM_N + g * SSM_N:D_SSM + SSM_G * SSM_N + (g + 1) * SSM_N]
        cb = cg.astype(BF16)
        bb = bg.astype(BF16)
        gm = lax.dot_general(cb, bb, (((1,), (1,)), ((), ())), preferred_element_type=F32)
        st = st_ref[g]
        yoff = _dot(cb, st.astype(BF16))
        for jp in range(heads_per_group // 2):
            pair = g * (heads_per_group // 2) + jp
            sl = slice(pair * LANES, (pair + 1) * LANES)
            xg_pair = xg[:, sl]
            ypair = ein_x[:, sl] * yoff[:, jp * LANES:(jp + 1) * LANES]
            for hh in range(2):
                c = lane_lo + pair * 2 + hh
                seg = acs[:, c:c + 1] - acs_t[c:c + 1, :]
                decay = jnp.exp(jnp.where(tmask, seg, NEG))
                md = (gm * decay).astype(BF16)
                half = left if hh == 0 else jnp.logical_not(left)
                ypair = ypair + _dot(md, jnp.where(half, xg_pair, 0.0).astype(BF16))
            ys.append(ypair)
        gsl = slice(g * heads_per_group * SSM_HD, (g + 1) * heads_per_group * SSM_HD)
        st_ref[g] = st * cd_x[:, gsl] + _dot(bg.T.astype(BF16), xgd[:, gsl])
    return jnp.concatenate(ys, axis=1), xs


def _ssd_fwd_body(actx_ref, actm_ref, dtx_ref, dtm_ref, tri_ref, e_ref, biasv_ref, alogv_ref,
                  yf_ref, st_ref):
    c = pl.program_id(1)

    @pl.when(c == 0)
    def _():
        st_ref[...] = jnp.zeros_like(st_ref)

    is_meta = c == 0
    act = jnp.where(is_meta, actm_ref[0], actx_ref[...])
    dtr = jnp.where(is_meta, dtm_ref[...], dtx_ref[...])
    row = lax.broadcasted_iota(jnp.int32, (CHUNK, 1), 0)
    valid = jnp.logical_or(jnp.logical_not(is_meta), row >= NPAD)
    y, _ = _ssd_chunk(act, dtr, valid, tri_ref[...], e_ref[...], biasv_ref[...], alogv_ref[...],
                      0, st_ref, True)
    yf_ref[...] = y


def _ssd_bwd_body(act_ref, dt_ref, yf_ref, z_ref, tri_ref, e_ref, biasv_ref, alogv_ref,
                  dskip_ref, nw_ref, out_ref, st_ref):
    c = pl.program_id(1)

    @pl.when(c == 0)
    def _():
        st_ref[...] = jnp.zeros_like(st_ref)

    yb, xs = _ssd_chunk(act_ref[...], dt_ref[...], True, tri_ref[...], e_ref[...],
                        biasv_ref[...], alogv_ref[...], H_SSM, st_ref, False)
    y = yf_ref[...] + yb + xs * dskip_ref[...]
    out_ref[...] = _rms(y * _silu(z_ref[...]), nw_ref[...]).astype(BF16)


def _ssd(act, meta_act, dt_raw, dt_meta, z, consts, nb, seq):
    nc = seq // CHUNK
    tril, triu, e_f, e_b, biasv, alogv, dskip_x, ssm_nw = consts
    st_shape = pltpu.VMEM((SSM_G, SSM_N, (H_SSM // SSM_G) * SSM_HD), F32)

    xblk = lambda b, c: (b * nc + jnp.maximum(c - 1, 0), 0)
    y_f = pl.pallas_call(
        _ssd_fwd_body,
        out_shape=jax.ShapeDtypeStruct((nb * seq, D_SSM), F32),
        grid=(nb, nc + 1),
        in_specs=[
            pl.BlockSpec((CHUNK, CONV_DIM), xblk),
            pl.BlockSpec((1, CHUNK, CONV_DIM), lambda b, c: (b, 0, 0)),
            pl.BlockSpec((CHUNK, LANES), xblk),
            _const_spec((CHUNK, LANES)),
            _const_spec((CHUNK, CHUNK)), _const_spec((LANES, D_SSM)),
            _const_spec((1, LANES)), _const_spec((1, LANES)),
        ],
        out_specs=pl.BlockSpec((CHUNK, D_SSM), xblk),
        scratch_shapes=[st_shape],
        compiler_params=_params(("arbitrary", "arbitrary")), name="ssd_fwd",
    )(act, meta_act, dt_raw, dt_meta, tril, e_f, biasv, alogv)

    rblk = lambda b, c: (b * nc + nc - 1 - c, 0)
    return pl.pallas_call(
        _ssd_bwd_body,
        out_shape=jax.ShapeDtypeStruct((nb * seq, D_SSM), BF16),
        grid=(nb, nc),
        in_specs=[
            pl.BlockSpec((CHUNK, CONV_DIM), rblk),
            pl.BlockSpec((CHUNK, LANES), rblk),
            pl.BlockSpec((CHUNK, D_SSM), rblk),
            pl.BlockSpec((CHUNK, D_SSM), rblk),
            _const_spec((CHUNK, CHUNK)), _const_spec((LANES, D_SSM)),
            _const_spec((1, LANES)), _const_spec((1, LANES)),
            _const_spec((1, D_SSM)), _const_spec((1, D_SSM)),
        ],
        out_specs=pl.BlockSpec((CHUNK, D_SSM), rblk),
        scratch_shapes=[st_shape],
        compiler_params=_params(("arbitrary", "arbitrary")), name="ssd_bwd",
    )(act, dt_raw, y_f, z, triu, e_b, biasv, alogv, dskip_x, ssm_nw)


def _attn_body(qT_ref, k_ref, vT_ref, km_ref, vmT_ref, slope_ref, lq1_ref, lk1_ref, lq2_ref,
               lk2_ref, nw_ref, out_ref, qaug_ref, sa_ref, sb_ref, acc_ref, *, t, seq, unrolled):
    nkv = seq // t
    slope = slope_ref[0]
    slope_q = slope[:, 0:1]
    row = lax.broadcasted_iota(jnp.int32, (HEAD_W, 1), 0)
    frow = lax.broadcasted_iota(jnp.int32, (HEAD_W, t), 0)
    fcol = lax.broadcasted_iota(jnp.int32, (HEAD_W, t), 1)
    c_hi = (fcol // 256) * 256
    feat = jnp.where(frow < 2, 1.0,
                     jnp.where(frow == 2, -slope_q * c_hi.astype(F32),
                               jnp.where(frow == 3, -slope_q * (fcol - c_hi).astype(F32), 0.0)))
    feats = (feat.astype(BF16), (-feat).astype(BF16))

    def set_queries(qT, slot):
        zero = jnp.zeros_like(qT)
        qz = (jnp.where(row < ATT_DH, qT, zero), jnp.where(row >= ATT_DH, qT, zero))
        for sg in range(2):
            for mp in range(2):
                qaug_ref[slot, sg, mp] = jnp.concatenate([qz[mp], feats[sg]], axis=0)

    def key_feat(rows, rel, valid):
        lane = lax.broadcasted_iota(jnp.int32, (rows, LANES), 1)
        r_hi = (rel // 256) * 256
        f = jnp.where(lane == 0, slope * r_hi.astype(F32),
                      jnp.where(lane == 1, slope * (rel - r_hi).astype(F32),
                                jnp.where(lane < 4, 1.0, 0.0)))
        if valid is not None:
            f = jnp.where(jnp.logical_and(lane == 1, jnp.logical_not(valid)), NEG, f)
        return f.astype(BF16)

    kfeat = key_feat(t, lax.broadcasted_iota(jnp.int32, (t, LANES), 0), None)
    ones_rows = jnp.ones((ONES_ROWS, t), BF16)

    mrow = lax.broadcasted_iota(jnp.int32, (CHUNK, LANES), 0)
    kfeat_m = key_feat(CHUNK, mrow - CHUNK, mrow >= NPAD)
    ones_m = jnp.ones((ONES_ROWS, CHUNK), BF16)
    lam = (jnp.exp(jnp.sum(lq1_ref[...] * lk1_ref[...], axis=-1, keepdims=True))
           - jnp.exp(jnp.sum(lq2_ref[...] * lk2_ref[...], axis=-1, keepdims=True)) + LAM_INIT)
    slots = (sa_ref, sb_ref)

    def as_f32(x):
        return float(x) if isinstance(x, int) else x.astype(F32)

    def x_lhs(j):
        k0 = j * t if isinstance(j, int) else pl.multiple_of(j * t, t)
        return jnp.concatenate([k_ref[0, pl.ds(k0, t), :], kfeat], axis=1)

    def unit_shift(unit, i):
        if unit[0] == "x":
            d = unit[1] - i
            return slope_q * as_f32(-(abs(d) if isinstance(d, int) else jnp.abs(d)) * t)
        return slope_q * as_f32(-(i * t)) if unit[0] == "meta" else 0.0

    def unit_rows(unit):
        return CHUNK if unit[0] == "meta" else t

    def produce(unit, i, qslot, s_ref):
        if unit[0] == "x":
            j = unit[1]
            after = j > i
            sg = int(after) if isinstance(after, bool) else after.astype(jnp.int32)
            lhs = x_lhs(j)
            blocks = [_dot(lhs, qaug_ref[qslot, sg, mp]) for mp in range(2)]
        elif unit[0] == "meta":
            lhs = jnp.concatenate([km_ref[0], kfeat_m], axis=1)
            blocks = [_dot(lhs, qaug_ref[qslot, 0, mp]) for mp in range(2)]
        else:
            lhs = x_lhs(i)
            blocks = [jnp.minimum(_dot(lhs, qaug_ref[qslot, 0, mp]), _dot(lhs, qaug_ref[qslot, 1, mp]))
                      for mp in range(2)]
        rows = unit_rows(unit)
        mbs = []
        for mp, s in enumerate(blocks):
            s_ref[mp, 0:rows, :] = s
            mbs.append(jnp.max(s, axis=0, keepdims=True))
        return tuple(mbs)

    def consume(unit, i, s_ref, mbs, ms):
        cterm = unit_shift(unit, i)
        rows = unit_rows(unit)
        if unit[0] == "meta":
            v_aug = jnp.concatenate([vmT_ref[0, 0], ones_m], axis=0)
        else:
            v_aug = jnp.concatenate([vT_ref[0, unit[1] if unit[0] == "x" else i], ones_rows], axis=0)
        out = []
        for mp in range(2):
            m_new = jnp.maximum(ms[mp], mbs[mp] + cterm)
            p = jnp.exp(s_ref[mp, 0:rows, :] - (m_new - cterm)).astype(BF16)
            alpha = jnp.exp(ms[mp] - m_new)
            acc_ref[mp] = acc_ref[mp] * alpha + _dot(v_aug, p)
            out.append(m_new)
        return tuple(out)

    def start_tile():
        acc_ref[...] = jnp.zeros_like(acc_ref)
        return (jnp.full((1, t), NEG, F32),) * 2

    def finalize(rows):
        o0 = acc_ref[0, 0:HEAD_W, :] / acc_ref[0, HEAD_W:HEAD_W + 1, :]
        o1 = acc_ref[1, 0:HEAD_W, :] / acc_ref[1, HEAD_W:HEAD_W + 1, :]
        o = o0 - lam * o1
        on = o * lax.rsqrt(jnp.mean(o * o, axis=0, keepdims=True) + EPS)
        out_ref[rows, :] = ((on.T * nw_ref[...]) * (1.0 - LAM_INIT)).astype(BF16)

    def run_static(units, pending, ms):
        for u, (i, unit) in enumerate(units):
            nxt = None
            if u + 1 < len(units):
                ni, nunit = units[u + 1]
                if unrolled and ni != i:
                    set_queries(qT_ref[0, :, ni * t:(ni + 1) * t], ni % 2)
                nxt = produce(nunit, ni, ni % 2 if unrolled else 0, slots[(u + 1) % 2])
            if unrolled and (u == 0 or units[u - 1][0] != i):
                ms = start_tile()
            ms = consume(unit, i, slots[u % 2], pending, ms)
            pending = nxt
            if unrolled and (u + 1 == len(units) or units[u + 1][0] != i):
                finalize(slice(i * t, (i + 1) * t))
        return ms

    tail = [("meta",), ("diag",)]
    if unrolled:
        units = []
        for i in range(nkv):
            units += [(i, ("x", n + (n >= i))) for n in range(nkv - 1)] + [(i, u) for u in tail]
        set_queries(qT_ref[0, :, 0:t], 0)
        run_static(units, produce(units[0][1], 0, 0, sa_ref), None)
    else:
        i = pl.program_id(2)
        set_queries(qT_ref[0], 0)
        ms = start_tile()
        n_off = nkv - 1
        xunit = lambda n: ("x", n + (n >= i).astype(jnp.int32))
        rest = tail
        if n_off >= 1:
            mbs = produce(xunit(0), i, 0, sa_ref)
            pairs = (n_off - 1) // 2

            def pair_body(u, carry):
                ms, mbs = carry[:2], carry[2:]
                n = 2 * u
                nxt = produce(xunit(n + 1), i, 0, sb_ref)
                ms = consume(xunit(n), i, sa_ref, mbs, ms)
                mbs = produce(xunit(n + 2), i, 0, sa_ref)
                ms = consume(xunit(n + 1), i, sb_ref, nxt, ms)
                return ms + mbs

            carry = lax.fori_loop(0, pairs, pair_body, ms + mbs, unroll=3 if pairs % 3 == 0 else 1)
            ms, mbs = carry[:2], carry[2:]
            rest = [xunit(n) for n in range(2 * pairs, n_off)] + tail
        else:
            mbs = produce(rest[0], i, 0, sa_ref)
        run_static([(i, u) for u in rest], mbs, ms)
        finalize(slice(None))


def _attention(qT, k, vT, k_meta, vT_meta, slopes, lq1, lk1, lq2, lk2, attn_nw, nb, seq):
    t = min(ROW_TILE, seq)
    nq = seq // t
    unrolled = nq * (nq + 1) <= ATTN_UNROLL_MAX_UNITS
    if unrolled:
        grid = (N_HEADS, nb)
        tq, q_map, out_map = seq, (lambda h, b: (h, 0, b)), (lambda h, b: (b, h))
        per_seq = lambda f: (lambda h, b: f(h, b))
    else:
        grid = (N_HEADS, nb, nq)
        tq = t
        q_map, out_map = (lambda h, b, i: (h, 0, b * nq + i)), (lambda h, b, i: (b * nq + i, h))
        per_seq = lambda f: (lambda h, b, i: f(h, b))
    vec = lambda n: _const_spec((1, n))
    in_specs = [
        pl.BlockSpec((1, HEAD_W, tq), q_map),
        pl.BlockSpec((1, seq, HEAD_W), per_seq(lambda h, b: (h, b, 0))),
        pl.BlockSpec((1, nq, HEAD_W, t), per_seq(lambda h, b: (h, b, 0, 0))),
        pl.BlockSpec((1, CHUNK, HEAD_W), per_seq(lambda h, b: (h, 0, 0))),
        pl.BlockSpec((1, 1, HEAD_W, CHUNK), per_seq(lambda h, b: (h, 0, 0, 0))),
        pl.BlockSpec((1, 1, LANES), per_seq(lambda h, b: (h, 0, 0))),
        vec(ATT_DH), vec(ATT_DH), vec(ATT_DH), vec(ATT_DH), vec(HEAD_W),
    ]
    return pl.pallas_call(
        functools.partial(_attn_body, t=t, seq=seq, unrolled=unrolled),
        out_shape=jax.ShapeDtypeStruct((nb * seq, ATT_VDIM), BF16),
        grid=grid, in_specs=in_specs,
        out_specs=pl.BlockSpec((tq, HEAD_W), out_map),
        scratch_shapes=[pltpu.VMEM((2, 2, 2, 2 * HEAD_W, t), BF16),
                        pltpu.VMEM((2, t, t), F32), pltpu.VMEM((2, t, t), F32),
                        pltpu.VMEM((2, HEAD_W + ONES_ROWS, t), F32)],
        compiler_params=_params(("arbitrary",) * len(grid)), name="attn",
    )(qT, k, vT, k_meta, vT_meta, slopes, lq1, lk1, lq2, lk2, attn_nw)


def _mlp_body(x_ref, att_ref, ssm_ref, wo_ref, n2_ref, wup_ref, wdown_ref, fw_ref, out_ref):
    h1 = (x_ref[...] + _dot(att_ref[...], wo_ref[0:ATT_VDIM, :])
          + _dot(ssm_ref[...], wo_ref[ATT_VDIM:ATT_VDIM + D_SSM, :]))
    u2 = _rms(h1, n2_ref[...]).astype(BF16)
    hid = jnp.maximum(_dot(u2, wup_ref[...]), 0.0)
    h2 = h1 + _dot((hid * hid).astype(BF16), wdown_ref[...])
    out_ref[...] = _rms(h2, fw_ref[...])


def _mlp(x2d, att, ssm, wo, n2, wup, wdown, fw, tm):
    rows = x2d.shape[0]
    tile = lambda w: pl.BlockSpec((tm, w), lambda i: (i, 0))
    return pl.pallas_call(
        _mlp_body, out_shape=jax.ShapeDtypeStruct((rows, D_MODEL), F32), grid=(rows // tm,),
        in_specs=[tile(D_MODEL), tile(ATT_VDIM), tile(D_SSM), _const_spec(wo.shape),
                  _const_spec((1, D_MODEL)), _const_spec(wup.shape), _const_spec(wdown.shape),
                  _const_spec((1, D_MODEL))],
        out_specs=tile(D_MODEL),
        compiler_params=_params(("arbitrary",)), name="mlp",
    )(x2d, att, ssm, wo, n2, wup, wdown, fw)


def _head_lane_vec(fwd_vals, bwd_vals):
    v = jnp.zeros((1, LANES), F32)
    v = v.at[0, 0:H_SSM].set(fwd_vals.astype(F32))
    return v.at[0, H_SSM:2 * H_SSM].set(bwd_vals.astype(F32))


def _expand_matrix(lane_lo):
    lane = jnp.arange(LANES)[:, None]
    ch = jnp.arange(D_SSM)[None, :] // SSM_HD
    return (lane == ch + lane_lo).astype(BF16)


def _trunk(x, meta_parts, p):
    nb, seq, _ = x.shape
    x2d = x.reshape(nb * seq, D_MODEL)
    k_meta, vT_meta, xbc_meta, dt_meta = meta_parts
    tm = min(ROW_TILE, seq)
    qT, k, vT, z, xbc_raw, dt_raw = _inproj(x2d, p["n1"], *p["w_in"], tm)
    act, meta_act = _conv(xbc_raw, xbc_meta, p["conv_w8"], p["conv_b"], nb, seq, tm)
    ssm = _ssd(act, meta_act, dt_raw, dt_meta, z, p["ssd_consts"], nb, seq)
    att = _attention(qT, k, vT, k_meta, vT_meta, p["slopes"], p["lq1"], p["lk1"], p["lq2"],
                     p["lk2"], p["attn_nw"], nb, seq)
    y = _mlp(x2d, att, ssm, p["wo"], p["n2"], p["wup"], p["wdown"], p["fw"], tm)
    return y.reshape(nb, seq, D_MODEL)


@jax.jit
def kernel(x_prompt, x_sample, meta_tokens, norm1_w, w_in, conv_w, conv_b, lambda_q1, lambda_k1,
           lambda_q2, lambda_k2, attn_norm_w, dt_bias_f, dt_bias_b, a_log_f, a_log_b, d_skip,
           ssm_norm_w, w_out, norm2_w, w_up, w_down, final_norm_w):
    li = 0
    w = w_in[li].astype(BF16)
    cuts = (0, ATT_VDIM, 2 * ATT_VDIM, 3 * ATT_VDIM, 3 * ATT_VDIM + D_SSM,
            3 * ATT_VDIM + D_SSM + CONV_DIM)
    wdt = jnp.zeros((D_MODEL, LANES), BF16).at[:, 0:2 * H_SSM].set(w[:, cuts[5]:])
    t = jnp.arange(CHUNK)
    p = {
        "n1": norm1_w[li].reshape(1, D_MODEL).astype(F32),
        "w_in": tuple(w[:, cuts[j]:cuts[j + 1]] for j in range(5)) + (wdt,),
        "conv_w8": jnp.zeros((HALO, CONV_DIM), F32).at[0:D_CONV].set(conv_w[li].astype(F32)),
        "conv_b": conv_b[li].reshape(1, CONV_DIM).astype(F32),
        "ssd_consts": (
            (t[:, None] >= t[None, :]).astype(BF16),
            (t[None, :] >= t[:, None]).astype(BF16),
            _expand_matrix(0), _expand_matrix(H_SSM),
            _head_lane_vec(dt_bias_f[li], dt_bias_b[li]),
            _head_lane_vec(a_log_f[li], a_log_b[li]),
            jnp.repeat(d_skip[li].astype(F32), SSM_HD).reshape(1, D_SSM),
            ssm_norm_w[li].reshape(1, D_SSM).astype(F32),
        ),
        "slopes": jnp.broadcast_to(
            (2.0 ** (-8.0 * (jnp.arange(N_HEADS, dtype=F32) + 1.0) / N_HEADS))[:, None, None],
            (N_HEADS, 1, LANES)),
        "lq1": lambda_q1[li].reshape(1, ATT_DH).astype(F32),
        "lk1": lambda_k1[li].reshape(1, ATT_DH).astype(F32),
        "lq2": lambda_q2[li].reshape(1, ATT_DH).astype(F32),
        "lk2": lambda_k2[li].reshape(1, ATT_DH).astype(F32),
        "attn_nw": attn_norm_w[li].reshape(1, HEAD_W).astype(F32),
        "wo": w_out[li].astype(BF16),
        "n2": norm2_w[li].reshape(1, D_MODEL).astype(F32),
        "wup": w_up[li].astype(BF16),
        "wdown": w_down[li].astype(BF16),
        "fw": final_norm_w.reshape(1, D_MODEL).astype(F32),
    }
    meta_block = jnp.concatenate(
        [jnp.zeros((NPAD, D_MODEL), F32), meta_tokens.astype(F32)], axis=0)
    _, k_m, vT_m, _, xbc_m, dt_m = _inproj(meta_block, p["n1"], *p["w_in"], CHUNK)
    meta_parts = (k_m, vT_m, xbc_m, dt_m)
    return (_trunk(x_prompt, meta_parts, p), _trunk(x_sample, meta_parts, p))
```

```python
import functools
import math

import jax
import jax.numpy as jnp
from jax import lax
from jax.experimental import pallas as pl
from jax.experimental.pallas import tpu as pltpu

F32 = jnp.float32
BF16 = jnp.bfloat16

D_MODEL = 1024
N_META = 16
CHUNK = 128
NPAD = CHUNK - N_META
N_HEADS = 8
ATT_DH = 64
HEAD_W = 2 * ATT_DH
ATT_VDIM = N_HEADS * HEAD_W
D_SSM = 1024
H_SSM = 16
SSM_HD = 64
SSM_G = 2
SSM_N = 128
D_CONV = 5
CONV_DIM = D_SSM + 2 * SSM_G * SSM_N
D_FF = 4 * D_MODEL
EPS = 1e-5
LAM_INIT = 0.8 - 0.6 * math.exp(-0.3 * 0)
NEG = -1e30

LANES = 128
HALO = 8
VMEM_LIMIT = 56 * 1024 * 1024

ROW_TILE = 512
ONES_ROWS = 16
SSD_CHUNKS_PER_STEP = 2
ATTN_UNROLL_MAX_UNITS = 24


def _dot(a, b):
    return jnp.dot(a, b, preferred_element_type=F32)


def _rms(x, w):
    return x * lax.rsqrt(jnp.mean(x * x, axis=-1, keepdims=True) + EPS) * w


def _silu(x):
    return x / (1.0 + jnp.exp(-x))


def _const_spec(shape):
    nd = len(shape)
    return pl.BlockSpec(shape, lambda *_: (0,) * nd)


def _params(sem, flags=None):
    return pltpu.CompilerParams(dimension_semantics=sem, vmem_limit_bytes=VMEM_LIMIT, flags=flags)


def _conv_taps(ext, w, b, rows):
    acc = b + w[0:1, :] * ext[HALO - 2:HALO - 2 + rows, :]
    for j in range(1, D_CONV):
        acc = acc + w[j:j + 1, :] * ext[HALO - 2 + j:HALO - 2 + j + rows, :]
    return _silu(acc)


def _project(u, wq_ref, wk_ref, wv_ref, wz_ref, wdt_ref, qT_ref, k_ref, vT_ref, z_ref, dt_ref):
    q = _dot(u, wq_ref[...]) * (ATT_DH ** -0.5)
    for h in range(N_HEADS):
        qT_ref[h] = q[:, h * HEAD_W:(h + 1) * HEAD_W].T.astype(BF16)
    k = _dot(u, wk_ref[...])
    for h in range(N_HEADS):
        k_ref[h] = k[:, h * HEAD_W:(h + 1) * HEAD_W].astype(BF16)
    v = _dot(u, wv_ref[...])
    for h in range(N_HEADS):
        vT_ref[h, 0] = v[:, h * HEAD_W:(h + 1) * HEAD_W].T.astype(BF16)
    z_ref[...] = _dot(u, wz_ref[...])
    dt_ref[...] = _dot(u, wdt_ref[...])


def _inproj_meta_body(x_ref, n1_ref, wq_ref, wk_ref, wv_ref, wz_ref, wx_ref, wdt_ref,
                      qT_ref, k_ref, vT_ref, z_ref, xbc_ref, dt_ref):
    u = _rms(x_ref[...], n1_ref[...]).astype(BF16)
    _project(u, wq_ref, wk_ref, wv_ref, wz_ref, wdt_ref, qT_ref, k_ref, vT_ref, z_ref, dt_ref)
    xbc_ref[...] = _dot(u, wx_ref[...])


def _inproj_body(x_ref, xprev_ref, xnext_ref, metaraw_ref, n1_ref, wq_ref, wk_ref, wv_ref, wz_ref,
                 wx_ref, wdt_ref, cw_ref, cb_ref,
                 qT_ref, k_ref, vT_ref, z_ref, act_ref, dt_ref, metaact_ref, *, tm, nt):
    i = pl.program_id(1)
    n1 = n1_ref[...]
    x_ext = jnp.concatenate([xprev_ref[...], x_ref[...], xnext_ref[...]], axis=0)
    u_ext = _rms(x_ext, n1).astype(BF16)
    _project(u_ext[HALO:HALO + tm, :], wq_ref, wk_ref, wv_ref, wz_ref, wdt_ref,
             qT_ref, k_ref, vT_ref, z_ref, dt_ref)
    raw = _dot(u_ext, wx_ref[...])
    w = cw_ref[...]
    b = cb_ref[...]
    prev = jnp.where(i == 0, metaraw_ref[CHUNK - HALO:CHUNK, :], raw[0:HALO, :])
    nxt = jnp.where(i == nt - 1, 0.0, raw[HALO + tm:, :])
    main = raw[HALO:HALO + tm, :]
    act_ref[...] = _conv_taps(jnp.concatenate([prev, main, nxt], axis=0), w, b, tm)

    @pl.when(i == 0)
    def _():
        zeros = jnp.zeros((HALO, CONV_DIM), F32)
        extm = jnp.concatenate([zeros, metaraw_ref[...], main[0:HALO, :]], axis=0)
        am = _conv_taps(extm, w, b, CHUNK)
        row = lax.broadcasted_iota(jnp.int32, (CHUNK, 1), 0)
        metaact_ref[0] = jnp.where(row >= NPAD, am, 0.0)


def _inproj_specs(rows, tm, tile_map):
    out_shape = (
        jax.ShapeDtypeStruct((N_HEADS, HEAD_W, rows), BF16),
        jax.ShapeDtypeStruct((N_HEADS, rows, HEAD_W), BF16),
        jax.ShapeDtypeStruct((N_HEADS, rows // tm, HEAD_W, tm), BF16),
        jax.ShapeDtypeStruct((rows, D_SSM), F32),
        jax.ShapeDtypeStruct((rows, CONV_DIM), F32),
        jax.ShapeDtypeStruct((rows, LANES), F32),
    )
    at = lambda f: (lambda *g: f(tile_map(*g)))
    out_specs = (
        pl.BlockSpec((N_HEADS, HEAD_W, tm), at(lambda r: (0, 0, r))),
        pl.BlockSpec((N_HEADS, tm, HEAD_W), at(lambda r: (0, r, 0))),
        pl.BlockSpec((N_HEADS, 1, HEAD_W, tm), at(lambda r: (0, r, 0, 0))),
        pl.BlockSpec((tm, D_SSM), at(lambda r: (r, 0))),
        pl.BlockSpec((tm, CONV_DIM), at(lambda r: (r, 0))),
        pl.BlockSpec((tm, LANES), at(lambda r: (r, 0))),
    )
    return out_shape, out_specs


def _inproj_meta(meta_block, n1, w_in):
    out_shape, out_specs = _inproj_specs(CHUNK, CHUNK, lambda i: i)
    return pl.pallas_call(
        _inproj_meta_body, out_shape=out_shape, grid=(1,),
        in_specs=[pl.BlockSpec((CHUNK, D_MODEL), lambda i: (i, 0)), _const_spec((1, D_MODEL))]
        + [_const_spec(w.shape) for w in w_in],
        out_specs=out_specs, compiler_params=_params(("arbitrary",)), name="inproj_meta",
    )(meta_block, n1, *w_in)


def _inproj(x2d, meta_raw, n1, w_in, conv_w8, conv_b, nb, seq, tm):
    nt = seq // tm
    hb = tm // HALO
    nblk8 = (nb * seq) // HALO
    out_shape, out_specs = _inproj_specs(nb * seq, tm, lambda b, i: b * nt + i)
    out_shape += (jax.ShapeDtypeStruct((nb, CHUNK, CONV_DIM), F32),)
    out_specs += (pl.BlockSpec((1, CHUNK, CONV_DIM), lambda b, i: (b, 0, 0)),)
    in_specs = [
        pl.BlockSpec((tm, D_MODEL), lambda b, i: (b * nt + i, 0)),
        pl.BlockSpec((HALO, D_MODEL), lambda b, i: (jnp.maximum((b * nt + i) * hb - 1, 0), 0)),
        pl.BlockSpec((HALO, D_MODEL),
                     lambda b, i: (jnp.minimum((b * nt + i + 1) * hb, nblk8 - 1), 0)),
        _const_spec((CHUNK, CONV_DIM)), _const_spec((1, D_MODEL)),
    ] + [_const_spec(w.shape) for w in w_in] + [_const_spec((HALO, CONV_DIM)),
                                                 _const_spec((1, CONV_DIM))]
    return pl.pallas_call(
        functools.partial(_inproj_body, tm=tm, nt=nt), out_shape=out_shape, grid=(nb, nt),
        in_specs=in_specs, out_specs=out_specs,
        compiler_params=_params(("arbitrary", "arbitrary")), name="inproj",
    )(x2d, x2d, x2d, meta_raw, n1, *w_in, conv_w8, conv_b)


def _split3(x):
    hi = x.astype(BF16)
    r = x - hi.astype(F32)
    mid = r.astype(BF16)
    lo = (r - mid.astype(F32)).astype(BF16)
    return hi, mid, lo


def _expand_heads(x, e):
    hi = x.astype(BF16)
    lo = (x - hi.astype(F32)).astype(BF16)
    return _dot(hi, e) + _dot(lo, e)


def _ssd_chunk(act, dtr, valid, tri, e, biasv, alogv, lane_lo, st_ref, fwd, want_y=True):
    xs = act[:, :D_SSM]
    lane = lax.broadcasted_iota(jnp.int32, (1, LANES), 1)
    hmask = (lane >= lane_lo) & (lane < lane_lo + H_SSM)
    x = dtr + biasv
    dt = jnp.maximum(x, 0.0) + jnp.log(1.0 + jnp.exp(-jnp.abs(x)))
    dt = jnp.where(hmask & valid, dt, 0.0)
    adt = dt * (-jnp.exp(alogv))
    hi, mid, lo = _split3(adt)
    acs = _dot(tri, hi) + _dot(tri, mid) + _dot(tri, lo)
    acs_t = acs.T
    edge = acs[CHUNK - 1:CHUNK, :] if fwd else acs[0:1, :]
    dt_x = _expand_heads(dt, e)
    ein_x = _expand_heads(jnp.exp(acs), e)
    est_x = _expand_heads(jnp.exp(edge - acs), e)
    cd_x = ein_x[CHUNK - 1:CHUNK, :] if fwd else ein_x[0:1, :]
    xg = xs * dt_x
    xgd = (xg * est_x).astype(BF16)
    li = lax.broadcasted_iota(jnp.int32, (CHUNK, CHUNK), 0)
    si = lax.broadcasted_iota(jnp.int32, (CHUNK, CHUNK), 1)
    tmask = (li >= si) if fwd else (si >= li)
    left = lax.broadcasted_iota(jnp.int32, (1, LANES), 1) < SSM_HD
    ys = []
    heads_per_group = H_SSM // SSM_G
    for g in range(SSM_G):
        bg = act[:, D_SSM + g * SSM_N:D_SSM + (g + 1) * SSM_N]
        cg = act[:, D_SSM + SSM_G * SSM_N + g * SSM_N:D_SSM + SSM_G * SSM_N + (g + 1) * SSM_N]
        cb = cg.astype(BF16)
        bb = bg.astype(BF16)
        st = st_ref[g]
        gsl = slice(g * heads_per_group * SSM_HD, (g + 1) * heads_per_group * SSM_HD)
        st_ref[g] = st * cd_x[:, gsl] + _dot(bg.T.astype(BF16), xgd[:, gsl])
        if not want_y:
            continue
        gm = lax.dot_general(cb, bb, (((1,), (1,)), ((), ())), preferred_element_type=F32)
        yoff = _dot(cb, st.astype(BF16))
        for jp in range(heads_per_group // 2):
            pair = g * (heads_per_group // 2) + jp
            sl = slice(pair * LANES, (pair + 1) * LANES)
            xg_pair = xg[:, sl]
            ypair = ein_x[:, sl] * yoff[:, jp * LANES:(jp + 1) * LANES]
            for hh in range(2):
                c = lane_lo + pair * 2 + hh
                seg = acs[:, c:c + 1] - acs_t[c:c + 1, :]
                decay = jnp.exp(jnp.where(tmask, seg, NEG))
                md = (gm * decay).astype(BF16)
                half = left if hh == 0 else jnp.logical_not(left)
                ypair = ypair + _dot(md, jnp.where(half, xg_pair, 0.0).astype(BF16))
            ys.append(ypair)
    return (jnp.concatenate(ys, axis=1) if want_y else None), xs


def _ssd_fwd_body(actx_ref, actm_ref, dtx_ref, dtm_ref, tri_ref, e_ref, biasv_ref, alogv_ref,
                  yf_ref, st_ref):
    consts = (tri_ref[...], e_ref[...], biasv_ref[...], alogv_ref[...], 0, st_ref, True)

    @pl.when(pl.program_id(1) == 0)
    def _():
        st_ref[...] = jnp.zeros_like(st_ref)
        row = lax.broadcasted_iota(jnp.int32, (CHUNK, 1), 0)
        _ssd_chunk(actm_ref[0], dtm_ref[...], row >= NPAD, *consts, want_y=False)

    for sub in range(SSD_CHUNKS_PER_STEP):
        rows = slice(sub * CHUNK, (sub + 1) * CHUNK)
        y, _ = _ssd_chunk(actx_ref[rows, :], dtx_ref[rows, :], True, *consts)
        yf_ref[rows, :] = y


def _ssd_bwd_body(act_ref, dt_ref, yf_ref, z_ref, tri_ref, e_ref, biasv_ref, alogv_ref,
                  dskip_ref, nw_ref, out_ref, st_ref):
    @pl.when(pl.program_id(1) == 0)
    def _():
        st_ref[...] = jnp.zeros_like(st_ref)

    for sub in reversed(range(SSD_CHUNKS_PER_STEP)):
        rows = slice(sub * CHUNK, (sub + 1) * CHUNK)
        yb, xs = _ssd_chunk(act_ref[rows, :], dt_ref[rows, :], True, tri_ref[...], e_ref[...],
                            biasv_ref[...], alogv_ref[...], H_SSM, st_ref, False)
        y = yf_ref[rows, :] + yb + xs * dskip_ref[...]
        out_ref[rows, :] = _rms(y * _silu(z_ref[rows, :]), nw_ref[...]).astype(BF16)


def _ssd(act, meta_act, dt_raw, dt_meta, z, consts, nb, seq):
    step = SSD_CHUNKS_PER_STEP * CHUNK
    ns = seq // step
    tril, triu, e_f, e_b, biasv, alogv, dskip_x, ssm_nw = consts
    st_shape = pltpu.VMEM((SSM_G, SSM_N, (H_SSM // SSM_G) * SSM_HD), F32)

    xblk = lambda b, c: (b * ns + c, 0)
    y_f = pl.pallas_call(
        _ssd_fwd_body,
        out_shape=jax.ShapeDtypeStruct((nb * seq, D_SSM), F32),
        grid=(nb, ns),
        in_specs=[
            pl.BlockSpec((step, CONV_DIM), xblk),
            pl.BlockSpec((1, CHUNK, CONV_DIM), lambda b, c: (b, 0, 0)),
            pl.BlockSpec((step, LANES), xblk),
            _const_spec((CHUNK, LANES)),
            _const_spec((CHUNK, CHUNK)), _const_spec((LANES, D_SSM)),
            _const_spec((1, LANES)), _const_spec((1, LANES)),
        ],
        out_specs=pl.BlockSpec((step, D_SSM), xblk),
        scratch_shapes=[st_shape],
        compiler_params=_params(("arbitrary", "arbitrary")), name="ssd_fwd",
    )(act, meta_act, dt_raw, dt_meta, tril, e_f, biasv, alogv)

    rblk = lambda b, c: (b * ns + ns - 1 - c, 0)
    return pl.pallas_call(
        _ssd_bwd_body,
        out_shape=jax.ShapeDtypeStruct((nb * seq, D_SSM), BF16),
        grid=(nb, ns),
        in_specs=[
            pl.BlockSpec((step, CONV_DIM), rblk),
            pl.BlockSpec((step, LANES), rblk),
            pl.BlockSpec((step, D_SSM), rblk),
            pl.BlockSpec((step, D_SSM), rblk),
            _const_spec((CHUNK, CHUNK)), _const_spec((LANES, D_SSM)),
            _const_spec((1, LANES)), _const_spec((1, LANES)),
            _const_spec((1, D_SSM)), _const_spec((1, D_SSM)),
        ],
        out_specs=pl.BlockSpec((step, D_SSM), rblk),
        scratch_shapes=[st_shape],
        compiler_params=_params(("arbitrary", "arbitrary")), name="ssd_bwd",
    )(act, dt_raw, y_f, z, triu, e_b, biasv, alogv, dskip_x, ssm_nw)


def _attn_body(qT_ref, k_ref, vT_ref, km_ref, vmT_ref, slope_ref, lq1_ref, lk1_ref, lq2_ref,
               lk2_ref, nw_ref, out_ref, qaug_ref, sa_ref, sb_ref, acc_ref, dbias_ref,
               *, t, seq, unrolled):
    nkv = seq // t
    slope = slope_ref[0]
    slope_q = slope[:, 0:1]
    row = lax.broadcasted_iota(jnp.int32, (HEAD_W, 1), 0)
    frow = lax.broadcasted_iota(jnp.int32, (HEAD_W, t), 0)
    fcol = lax.broadcasted_iota(jnp.int32, (HEAD_W, t), 1)
    c_hi = (fcol // 256) * 256
    feat = jnp.where(frow < 2, 1.0,
                     jnp.where(frow == 2, -slope_q * c_hi.astype(F32),
                               jnp.where(frow == 3, -slope_q * (fcol - c_hi).astype(F32), 0.0)))
    feats = (feat.astype(BF16), (-feat).astype(BF16))

    first_of_head = pl.program_id(1) == 0
    if not unrolled:
        first_of_head = jnp.logical_and(first_of_head, pl.program_id(2) == 0)

    @pl.when(first_of_head)
    def _():
        d = (lax.broadcasted_iota(jnp.int32, (t, t), 0)
             - lax.broadcasted_iota(jnp.int32, (t, t), 1))
        dbias_ref[...] = jnp.where(d > 0, (-2.0 * slope_q) * d.astype(F32), 0.0)

    def set_queries(qT, slot):
        zero = jnp.zeros_like(qT)
        qz = (jnp.where(row < ATT_DH, qT, zero), jnp.where(row >= ATT_DH, qT, zero))
        for sg in range(2):
            for mp in range(2):
                qaug_ref[slot, sg, mp] = jnp.concatenate([qz[mp], feats[sg]], axis=0)

    def key_feat(rows, rel, valid):
        lane = lax.broadcasted_iota(jnp.int32, (rows, LANES), 1)
        r_hi = (rel // 256) * 256
        f = jnp.where(lane == 0, slope * r_hi.astype(F32),
                      jnp.where(lane == 1, slope * (rel - r_hi).astype(F32),
                                jnp.where(lane < 4, 1.0, 0.0)))
        if valid is not None:
            f = jnp.where(jnp.logical_and(lane == 1, jnp.logical_not(valid)), NEG, f)
        return f.astype(BF16)

    kfeat = key_feat(t, lax.broadcasted_iota(jnp.int32, (t, LANES), 0), None)
    ones_rows = jnp.ones((ONES_ROWS, t), BF16)

    mrow = lax.broadcasted_iota(jnp.int32, (CHUNK, LANES), 0)
    kfeat_m = key_feat(CHUNK, mrow - CHUNK, mrow >= NPAD)
    ones_m = jnp.ones((ONES_ROWS, CHUNK), BF16)
    lam = (jnp.exp(jnp.sum(lq1_ref[...] * lk1_ref[...], axis=-1, keepdims=True))
           - jnp.exp(jnp.sum(lq2_ref[...] * lk2_ref[...], axis=-1, keepdims=True)) + LAM_INIT)
    slots = (sa_ref, sb_ref)

    def as_f32(x):
        return float(x) if isinstance(x, int) else x.astype(F32)

    def x_lhs(j):
        k0 = j * t if isinstance(j, int) else pl.multiple_of(j * t, t)
        return jnp.concatenate([k_ref[0, pl.ds(k0, t), :], kfeat], axis=1)

    def unit_shift(unit, i):
        if unit[0] == "x":
            d = unit[1] - i
            return slope_q * as_f32(-(abs(d) if isinstance(d, int) else jnp.abs(d)) * t)
        return slope_q * as_f32(-(i * t)) if unit[0] == "meta" else 0.0

    def unit_rows(unit):
        return CHUNK if unit[0] == "meta" else t

    def produce(unit, i, qslot, s_ref):
        if unit[0] == "x":
            j = unit[1]
            after = j > i
            sg = int(after) if isinstance(after, bool) else after.astype(jnp.int32)
            lhs = x_lhs(j)
            blocks = [_dot(lhs, qaug_ref[qslot, sg, mp]) for mp in range(2)]
        elif unit[0] == "meta":
            lhs = jnp.concatenate([km_ref[0], kfeat_m], axis=1)
            blocks = [_dot(lhs, qaug_ref[qslot, 0, mp]) for mp in range(2)]
        else:
            lhs = x_lhs(i)
            blocks = [_dot(lhs, qaug_ref[qslot, 0, mp]) + dbias_ref[...] for mp in range(2)]
        rows = unit_rows(unit)
        mbs = []
        for mp, s in enumerate(blocks):
            s_ref[mp, 0:rows, :] = s
            mbs.append(jnp.max(s, axis=0, keepdims=True))
        return tuple(mbs)

    def consume(unit, i, s_ref, mbs, ms):
        cterm = unit_shift(unit, i)
        rows = unit_rows(unit)
        if unit[0] == "meta":
            v_aug = jnp.concatenate([vmT_ref[0, 0], ones_m], axis=0)
        else:
            v_aug = jnp.concatenate([vT_ref[0, unit[1] if unit[0] == "x" else i], ones_rows], axis=0)
        out = []
        for mp in range(2):
            m_new = jnp.maximum(ms[mp], mbs[mp] + cterm)
            p = jnp.exp(s_ref[mp, 0:rows, :] - (m_new - cterm)).astype(BF16)
            alpha = jnp.exp(ms[mp] - m_new)
            acc_ref[mp] = acc_ref[mp] * alpha + _dot(v_aug, p)
            out.append(m_new)
        return tuple(out)

    def start_tile():
        acc_ref[...] = jnp.zeros_like(acc_ref)
        return (jnp.full((1, t), NEG, F32),) * 2

    def finalize(rows):
        o0 = acc_ref[0, 0:HEAD_W, :] / acc_ref[0, HEAD_W:HEAD_W + 1, :]
        o1 = acc_ref[1, 0:HEAD_W, :] / acc_ref[1, HEAD_W:HEAD_W + 1, :]
        o = o0 - lam * o1
        on = o * lax.rsqrt(jnp.mean(o * o, axis=0, keepdims=True) + EPS)
        out_ref[rows, :] = ((on.T * nw_ref[...]) * (1.0 - LAM_INIT)).astype(BF16)

    def run_static(units, pending, ms):
        for u, (i, unit) in enumerate(units):
            nxt = None
            if u + 1 < len(units):
                ni, nunit = units[u + 1]
                if unrolled and ni != i:
                    set_queries(qT_ref[0, :, ni * t:(ni + 1) * t], ni % 2)
                nxt = produce(nunit, ni, ni % 2 if unrolled else 0, slots[(u + 1) % 2])
            if unrolled and (u == 0 or units[u - 1][0] != i):
                ms = start_tile()
            ms = consume(unit, i, slots[u % 2], pending, ms)
            pending = nxt
            if unrolled and (u + 1 == len(units) or units[u + 1][0] != i):
                finalize(slice(i * t, (i + 1) * t))
        return ms

    tail = [("meta",), ("diag",)]
    if unrolled:
        units = []
        for i in range(nkv):
            units += [(i, ("x", n + (n >= i))) for n in range(nkv - 1)] + [(i, u) for u in tail]
        set_queries(qT_ref[0, :, 0:t], 0)
        run_static(units, produce(units[0][1], 0, 0, sa_ref), None)
    else:
        i = pl.program_id(2)
        set_queries(qT_ref[0], 0)
        ms = start_tile()
        n_off = nkv - 1
        xunit = lambda n: ("x", n + (n >= i).astype(jnp.int32))
        rest = tail
        if n_off >= 1:
            mbs = produce(xunit(0), i, 0, sa_ref)
            pairs = (n_off - 1) // 2

            def pair_body(u, carry):
                ms, mbs = carry[:2], carry[2:]
                n = 2 * u
                nxt = produce(xunit(n + 1), i, 0, sb_ref)
                ms = consume(xunit(n), i, sa_ref, mbs, ms)
                mbs = produce(xunit(n + 2), i, 0, sa_ref)
                ms = consume(xunit(n + 1), i, sb_ref, nxt, ms)
                return ms + mbs

            unroll = next((f for f in (5, 3) if pairs % f == 0), 1)
            carry = lax.fori_loop(0, pairs, pair_body, ms + mbs, unroll=unroll)
            ms, mbs = carry[:2], carry[2:]
            rest = [xunit(n) for n in range(2 * pairs, n_off)] + tail
        else:
            mbs = produce(rest[0], i, 0, sa_ref)
        run_static([(i, u) for u in rest], mbs, ms)
        finalize(slice(None))


def _attention(qT, k, vT, k_meta, vT_meta, slopes, lq1, lk1, lq2, lk2, attn_nw, nb, seq):
    t = min(ROW_TILE, seq)
    nq = seq // t
    unrolled = nq * (nq + 1) <= ATTN_UNROLL_MAX_UNITS
    if unrolled:
        grid = (N_HEADS, nb)
        tq, q_map, out_map = seq, (lambda h, b: (h, 0, b)), (lambda h, b: (b, h))
        per_seq = lambda f: (lambda h, b: f(h, b))
    else:
        grid = (N_HEADS, nb, nq)
        tq = t
        q_map, out_map = (lambda h, b, i: (h, 0, b * nq + i)), (lambda h, b, i: (b * nq + i, h))
        per_seq = lambda f: (lambda h, b, i: f(h, b))
    vec = lambda n: _const_spec((1, n))
    in_specs = [
        pl.BlockSpec((1, HEAD_W, tq), q_map),
        pl.BlockSpec((1, seq, HEAD_W), per_seq(lambda h, b: (h, b, 0))),
        pl.BlockSpec((1, nq, HEAD_W, t), per_seq(lambda h, b: (h, b, 0, 0))),
        pl.BlockSpec((1, CHUNK, HEAD_W), per_seq(lambda h, b: (h, 0, 0))),
        pl.BlockSpec((1, 1, HEAD_W, CHUNK), per_seq(lambda h, b: (h, 0, 0, 0))),
        pl.BlockSpec((1, 1, LANES), per_seq(lambda h, b: (h, 0, 0))),
        vec(ATT_DH), vec(ATT_DH), vec(ATT_DH), vec(ATT_DH), vec(HEAD_W),
    ]
    return pl.pallas_call(
        functools.partial(_attn_body, t=t, seq=seq, unrolled=unrolled),
        out_shape=jax.ShapeDtypeStruct((nb * seq, ATT_VDIM), BF16),
        grid=grid, in_specs=in_specs,
        out_specs=pl.BlockSpec((tq, HEAD_W), out_map),
        scratch_shapes=[pltpu.VMEM((2, 2, 2, 2 * HEAD_W, t), BF16),
                        pltpu.VMEM((2, t, t), F32), pltpu.VMEM((2, t, t), F32),
                        pltpu.VMEM((2, HEAD_W + ONES_ROWS, t), F32),
                        pltpu.VMEM((t, t), F32)],
        compiler_params=_params(("arbitrary",) * len(grid)), name="attn",
    )(qT, k, vT, k_meta, vT_meta, slopes, lq1, lk1, lq2, lk2, attn_nw)


def _mlp_body(x_ref, att_ref, ssm_ref, wo_ref, n2_ref, wup_ref, wdown_ref, fw_ref, out_ref):
    h1 = (x_ref[...] + _dot(att_ref[...], wo_ref[0:ATT_VDIM, :])
          + _dot(ssm_ref[...], wo_ref[ATT_VDIM:ATT_VDIM + D_SSM, :]))
    u2 = _rms(h1, n2_ref[...]).astype(BF16)
    hid = jnp.maximum(_dot(u2, wup_ref[...]), 0.0)
    h2 = h1 + _dot((hid * hid).astype(BF16), wdown_ref[...])
    out_ref[...] = _rms(h2, fw_ref[...])


def _mlp(x2d, att, ssm, wo, n2, wup, wdown, fw, tm):
    rows = x2d.shape[0]
    tile = lambda w: pl.BlockSpec((tm, w), lambda i: (i, 0))
    return pl.pallas_call(
        _mlp_body, out_shape=jax.ShapeDtypeStruct((rows, D_MODEL), F32), grid=(rows // tm,),
        in_specs=[tile(D_MODEL), tile(ATT_VDIM), tile(D_SSM), _const_spec(wo.shape),
                  _const_spec((1, D_MODEL)), _const_spec(wup.shape), _const_spec(wdown.shape),
                  _const_spec((1, D_MODEL))],
        out_specs=tile(D_MODEL),
        compiler_params=_params(("arbitrary",)), name="mlp",
    )(x2d, att, ssm, wo, n2, wup, wdown, fw)


def _head_lane_vec(fwd_vals, bwd_vals):
    v = jnp.zeros((1, LANES), F32)
    v = v.at[0, 0:H_SSM].set(fwd_vals.astype(F32))
    return v.at[0, H_SSM:2 * H_SSM].set(bwd_vals.astype(F32))


def _expand_matrix(lane_lo):
    lane = jnp.arange(LANES)[:, None]
    ch = jnp.arange(D_SSM)[None, :] // SSM_HD
    return (lane == ch + lane_lo).astype(BF16)


def _trunk(x, meta_parts, p):
    nb, seq, _ = x.shape
    x2d = x.reshape(nb * seq, D_MODEL)
    k_meta, vT_meta, xbc_meta, dt_meta = meta_parts
    tm = min(ROW_TILE, seq)
    qT, k, vT, z, act, dt_raw, meta_act = _inproj(
        x2d, xbc_meta, p["n1"], p["w_in"], p["conv_w8"], p["conv_b"], nb, seq, tm)
    ssm = _ssd(act, meta_act, dt_raw, dt_meta, z, p["ssd_consts"], nb, seq)
    att = _attention(qT, k, vT, k_meta, vT_meta, p["slopes"], p["lq1"], p["lk1"], p["lq2"],
                     p["lk2"], p["attn_nw"], nb, seq)
    y = _mlp(x2d, att, ssm, p["wo"], p["n2"], p["wup"], p["wdown"], p["fw"], tm)
    return y.reshape(nb, seq, D_MODEL)


@jax.jit
def kernel(x_prompt, x_sample, meta_tokens, norm1_w, w_in, conv_w, conv_b, lambda_q1, lambda_k1,
           lambda_q2, lambda_k2, attn_norm_w, dt_bias_f, dt_bias_b, a_log_f, a_log_b, d_skip,
           ssm_norm_w, w_out, norm2_w, w_up, w_down, final_norm_w):
    li = 0
    w = w_in[li].astype(BF16)
    cuts = (0, ATT_VDIM, 2 * ATT_VDIM, 3 * ATT_VDIM, 3 * ATT_VDIM + D_SSM,
            3 * ATT_VDIM + D_SSM + CONV_DIM)
    wdt = jnp.zeros((D_MODEL, LANES), BF16).at[:, 0:2 * H_SSM].set(w[:, cuts[5]:])
    t = jnp.arange(CHUNK)
    p = {
        "n1": norm1_w[li].reshape(1, D_MODEL).astype(F32),
        "w_in": tuple(w[:, cuts[j]:cuts[j + 1]] for j in range(5)) + (wdt,),
        "conv_w8": jnp.zeros((HALO, CONV_DIM), F32).at[0:D_CONV].set(conv_w[li].astype(F32)),
        "conv_b": conv_b[li].reshape(1, CONV_DIM).astype(F32),
        "ssd_consts": (
            (t[:, None] >= t[None, :]).astype(BF16),
            (t[None, :] >= t[:, None]).astype(BF16),
            _expand_matrix(0), _expand_matrix(H_SSM),
            _head_lane_vec(dt_bias_f[li], dt_bias_b[li]),
            _head_lane_vec(a_log_f[li], a_log_b[li]),
            jnp.repeat(d_skip[li].astype(F32), SSM_HD).reshape(1, D_SSM),
            ssm_norm_w[li].reshape(1, D_SSM).astype(F32),
        ),
        "slopes": jnp.broadcast_to(
            (2.0 ** (-8.0 * (jnp.arange(N_HEADS, dtype=F32) + 1.0) / N_HEADS))[:, None, None],
            (N_HEADS, 1, LANES)),
        "lq1": lambda_q1[li].reshape(1, ATT_DH).astype(F32),
        "lk1": lambda_k1[li].reshape(1, ATT_DH).astype(F32),
        "lq2": lambda_q2[li].reshape(1, ATT_DH).astype(F32),
        "lk2": lambda_k2[li].reshape(1, ATT_DH).astype(F32),
        "attn_nw": attn_norm_w[li].reshape(1, HEAD_W).astype(F32),
        "wo": w_out[li].astype(BF16),
        "n2": norm2_w[li].reshape(1, D_MODEL).astype(F32),
        "wup": w_up[li].astype(BF16),
        "wdown": w_down[li].astype(BF16),
        "fw": final_norm_w.reshape(1, D_MODEL).astype(F32),
    }
    meta_block = jnp.concatenate(
        [jnp.zeros((NPAD, D_MODEL), F32), meta_tokens.astype(F32)], axis=0)
    _, k_m, vT_m, _, xbc_m, dt_m = _inproj_meta(meta_block, p["n1"], p["w_in"])
    meta_parts = (k_m, vT_m, xbc_m, dt_m)
    return (_trunk(x_prompt, meta_parts, p), _trunk(x_sample, meta_parts, p))
```

```python
import functools
import math

import jax
import jax.numpy as jnp
from jax import lax
from jax.experimental import pallas as pl
from jax.experimental.pallas import tpu as pltpu

F32 = jnp.float32
BF16 = jnp.bfloat16

D_MODEL = 1024
N_META = 16
CHUNK = 128
NPAD = CHUNK - N_META
N_HEADS = 8
ATT_DH = 64
HEAD_W = 2 * ATT_DH
ATT_VDIM = N_HEADS * HEAD_W
D_SSM = 1024
H_SSM = 16
SSM_HD = 64
SSM_G = 2
SSM_N = 128
D_CONV = 5
CONV_DIM = D_SSM + 2 * SSM_G * SSM_N
D_FF = 4 * D_MODEL
EPS = 1e-5
LAM_INIT = 0.8 - 0.6 * math.exp(-0.3 * 0)
NEG = -1e30

LANES = 128
HALO = 8
VMEM_LIMIT = 56 * 1024 * 1024

ROW_TILE = 512
ONES_ROWS = 16
SSD_CHUNKS_PER_STEP = 2
ATTN_UNROLL_MAX_UNITS = 24


def _dot(a, b):
    return jnp.dot(a, b, preferred_element_type=F32)


def _rms(x, w):
    return x * lax.rsqrt(jnp.mean(x * x, axis=-1, keepdims=True) + EPS) * w


def _silu(x):
    return x / (1.0 + jnp.exp(-x))


def _const_spec(shape):
    nd = len(shape)
    return pl.BlockSpec(shape, lambda *_: (0,) * nd)


def _params(sem, flags=None):
    return pltpu.CompilerParams(dimension_semantics=sem, vmem_limit_bytes=VMEM_LIMIT, flags=flags)


def _conv_taps(ext, w, b, rows):
    acc = b + w[0:1, :] * ext[HALO - 2:HALO - 2 + rows, :]
    for j in range(1, D_CONV):
        acc = acc + w[j:j + 1, :] * ext[HALO - 2 + j:HALO - 2 + j + rows, :]
    return _silu(acc)


def _project(u, wq_ref, wk_ref, wv_ref, wz_ref, wdt_ref, qT_ref, k_ref, vT_ref, z_ref, dt_ref):
    q = _dot(u, wq_ref[...]) * (ATT_DH ** -0.5)
    for h in range(N_HEADS):
        qT_ref[h] = q[:, h * HEAD_W:(h + 1) * HEAD_W].T.astype(BF16)
    k = _dot(u, wk_ref[...])
    for h in range(N_HEADS):
        k_ref[h] = k[:, h * HEAD_W:(h + 1) * HEAD_W].astype(BF16)
    v = _dot(u, wv_ref[...])
    for h in range(N_HEADS):
        vT_ref[h, 0] = v[:, h * HEAD_W:(h + 1) * HEAD_W].T.astype(BF16)
    z_ref[...] = _dot(u, wz_ref[...])
    dt_ref[...] = _dot(u, wdt_ref[...])


def _inproj_meta_body(x_ref, n1_ref, wq_ref, wk_ref, wv_ref, wz_ref, wx_ref, wdt_ref,
                      qT_ref, k_ref, vT_ref, z_ref, xbc_ref, dt_ref):
    u = _rms(x_ref[...], n1_ref[...]).astype(BF16)
    _project(u, wq_ref, wk_ref, wv_ref, wz_ref, wdt_ref, qT_ref, k_ref, vT_ref, z_ref, dt_ref)
    xbc_ref[...] = _dot(u, wx_ref[...])


def _inproj_body(x_ref, xprev_ref, xnext_ref, metaraw_ref, n1_ref, wq_ref, wk_ref, wv_ref, wz_ref,
                 wx_ref, wdt_ref, cw_ref, cb_ref,
                 qT_ref, k_ref, vT_ref, z_ref, act_ref, dt_ref, metaact_ref, *, tm, nt):
    i = pl.program_id(1)
    n1 = n1_ref[...]
    x_ext = jnp.concatenate([xprev_ref[...], x_ref[...], xnext_ref[...]], axis=0)
    u_ext = _rms(x_ext, n1).astype(BF16)
    raw = _dot(u_ext, wx_ref[...])
    w = cw_ref[...]
    b = cb_ref[...]
    prev = jnp.where(i == 0, metaraw_ref[CHUNK - HALO:CHUNK, :], raw[0:HALO, :])
    nxt = jnp.where(i == nt - 1, 0.0, raw[HALO + tm:, :])
    main = raw[HALO:HALO + tm, :]
    act_ref[...] = _conv_taps(jnp.concatenate([prev, main, nxt], axis=0), w, b, tm)
    _project(u_ext[HALO:HALO + tm, :], wq_ref, wk_ref, wv_ref, wz_ref, wdt_ref,
             qT_ref, k_ref, vT_ref, z_ref, dt_ref)

    @pl.when(i == 0)
    def _():
        zeros = jnp.zeros((HALO, CONV_DIM), F32)
        extm = jnp.concatenate([zeros, metaraw_ref[...], main[0:HALO, :]], axis=0)
        am = _conv_taps(extm, w, b, CHUNK)
        row = lax.broadcasted_iota(jnp.int32, (CHUNK, 1), 0)
        metaact_ref[0] = jnp.where(row >= NPAD, am, 0.0)


def _inproj_specs(rows, tm, tile_map):
    out_shape = (
        jax.ShapeDtypeStruct((N_HEADS, HEAD_W, rows), BF16),
        jax.ShapeDtypeStruct((N_HEADS, rows, HEAD_W), BF16),
        jax.ShapeDtypeStruct((N_HEADS, rows // tm, HEAD_W, tm), BF16),
        jax.ShapeDtypeStruct((rows, D_SSM), F32),
        jax.ShapeDtypeStruct((rows, CONV_DIM), F32),
        jax.ShapeDtypeStruct((rows, LANES), F32),
    )
    at = lambda f: (lambda *g: f(tile_map(*g)))
    out_specs = (
        pl.BlockSpec((N_HEADS, HEAD_W, tm), at(lambda r: (0, 0, r))),
        pl.BlockSpec((N_HEADS, tm, HEAD_W), at(lambda r: (0, r, 0))),
        pl.BlockSpec((N_HEADS, 1, HEAD_W, tm), at(lambda r: (0, r, 0, 0))),
        pl.BlockSpec((tm, D_SSM), at(lambda r: (r, 0))),
        pl.BlockSpec((tm, CONV_DIM), at(lambda r: (r, 0))),
        pl.BlockSpec((tm, LANES), at(lambda r: (r, 0))),
    )
    return out_shape, out_specs


def _inproj_meta(meta_block, n1, w_in):
    out_shape, out_specs = _inproj_specs(CHUNK, CHUNK, lambda i: i)
    return pl.pallas_call(
        _inproj_meta_body, out_shape=out_shape, grid=(1,),
        in_specs=[pl.BlockSpec((CHUNK, D_MODEL), lambda i: (i, 0)), _const_spec((1, D_MODEL))]
        + [_const_spec(w.shape) for w in w_in],
        out_specs=out_specs, compiler_params=_params(("arbitrary",)), name="inproj_meta",
    )(meta_block, n1, *w_in)


def _inproj(x2d, meta_raw, n1, w_in, conv_w8, conv_b, nb, seq, tm):
    nt = seq // tm
    hb = tm // HALO
    nblk8 = (nb * seq) // HALO
    out_shape, out_specs = _inproj_specs(nb * seq, tm, lambda b, i: b * nt + i)
    out_shape += (jax.ShapeDtypeStruct((nb, CHUNK, CONV_DIM), F32),)
    out_specs += (pl.BlockSpec((1, CHUNK, CONV_DIM), lambda b, i: (b, 0, 0)),)
    in_specs = [
        pl.BlockSpec((tm, D_MODEL), lambda b, i: (b * nt + i, 0)),
        pl.BlockSpec((HALO, D_MODEL), lambda b, i: (jnp.maximum((b * nt + i) * hb - 1, 0), 0)),
        pl.BlockSpec((HALO, D_MODEL),
                     lambda b, i: (jnp.minimum((b * nt + i + 1) * hb, nblk8 - 1), 0)),
        _const_spec((CHUNK, CONV_DIM)), _const_spec((1, D_MODEL)),
    ] + [_const_spec(w.shape) for w in w_in] + [_const_spec((HALO, CONV_DIM)),
                                                 _const_spec((1, CONV_DIM))]
    return pl.pallas_call(
        functools.partial(_inproj_body, tm=tm, nt=nt), out_shape=out_shape, grid=(nb, nt),
        in_specs=in_specs, out_specs=out_specs,
        compiler_params=_params(("arbitrary", "arbitrary")), name="inproj",
    )(x2d, x2d, x2d, meta_raw, n1, *w_in, conv_w8, conv_b)


def _split3(x):
    hi = x.astype(BF16)
    r = x - hi.astype(F32)
    mid = r.astype(BF16)
    lo = (r - mid.astype(F32)).astype(BF16)
    return hi, mid, lo


def _expand_heads(x, e2):
    hi = x.astype(BF16)
    lo = (x - hi.astype(F32)).astype(BF16)
    return _dot(jnp.concatenate([hi, lo], axis=1), e2)


def _ssd_chunk(act, dtr, valid, tri, e, biasv, alogv, lane_lo, st_ref, fwd, want_y=True):
    xs = act[:, :D_SSM]
    lane = lax.broadcasted_iota(jnp.int32, (1, LANES), 1)
    hmask = (lane >= lane_lo) & (lane < lane_lo + H_SSM)
    x = dtr + biasv
    dt = jnp.maximum(x, 0.0) + jnp.log(1.0 + jnp.exp(-jnp.abs(x)))
    dt = jnp.where(hmask & valid, dt, 0.0)
    adt = dt * (-jnp.exp(alogv))
    hi, mid, lo = _split3(adt)
    acs = (_dot(jnp.concatenate([tri, tri], axis=1), jnp.concatenate([hi, mid], axis=0))
           + _dot(tri, lo))
    acs_t = acs.T
    edge = acs[CHUNK - 1:CHUNK, :] if fwd else acs[0:1, :]
    e2 = jnp.concatenate([e, e], axis=0)
    dt_x = _expand_heads(dt, e2)
    ein_x = _expand_heads(jnp.exp(acs), e2)
    est_x = _expand_heads(jnp.exp(edge - acs), e2)
    cd_x = ein_x[CHUNK - 1:CHUNK, :] if fwd else ein_x[0:1, :]
    xg = xs * dt_x
    xgd = (xg * est_x).astype(BF16)
    li = lax.broadcasted_iota(jnp.int32, (CHUNK, CHUNK), 0)
    si = lax.broadcasted_iota(jnp.int32, (CHUNK, CHUNK), 1)
    tmask = (li >= si) if fwd else (si >= li)
    left = lax.broadcasted_iota(jnp.int32, (1, LANES), 1) < SSM_HD
    ys = []
    heads_per_group = H_SSM // SSM_G
    for g in range(SSM_G):
        bg = act[:, D_SSM + g * SSM_N:D_SSM + (g + 1) * SSM_N]
        cg = act[:, D_SSM + SSM_G * SSM_N + g * SSM_N:D_SSM + SSM_G * SSM_N + (g + 1) * SSM_N]
        cb = cg.astype(BF16)
        bb = bg.astype(BF16)
        st = st_ref[g]
        gsl = slice(g * heads_per_group * SSM_HD, (g + 1) * heads_per_group * SSM_HD)
        st_ref[g] = st * cd_x[:, gsl] + _dot(bg.T.astype(BF16), xgd[:, gsl])
        if not want_y:
            continue
        gm = lax.dot_general(cb, bb, (((1,), (1,)), ((), ())), preferred_element_type=F32)
        yoff = _dot(cb, st.astype(BF16))
        for jp in range(heads_per_group // 2):
            pair = g * (heads_per_group // 2) + jp
            sl = slice(pair * LANES, (pair + 1) * LANES)
            xg_pair = xg[:, sl]
            mds = []
            for hh in range(2):
                c = lane_lo + pair * 2 + hh
                seg = acs[:, c:c + 1] - acs_t[c:c + 1, :]
                decay = jnp.exp(jnp.where(tmask, seg, NEG))
                mds.append((gm * decay).astype(BF16))
            xg2 = jnp.concatenate([jnp.where(left, xg_pair, 0.0).astype(BF16),
                                   jnp.where(left, 0.0, xg_pair).astype(BF16)], axis=0)
            ys.append(ein_x[:, sl] * yoff[:, jp * LANES:(jp + 1) * LANES]
                      + _dot(jnp.concatenate(mds, axis=1), xg2))
    return (jnp.concatenate(ys, axis=1) if want_y else None), xs


def _ssd_fwd_body(actx_ref, actm_ref, dtx_ref, dtm_ref, tri_ref, e_ref, biasv_ref, alogv_ref,
                  yf_ref, st_ref):
    consts = (tri_ref[...], e_ref[...], biasv_ref[...], alogv_ref[...], 0, st_ref, True)

    @pl.when(pl.program_id(1) == 0)
    def _():
        st_ref[...] = jnp.zeros_like(st_ref)
        row = lax.broadcasted_iota(jnp.int32, (CHUNK, 1), 0)
        _ssd_chunk(actm_ref[0], dtm_ref[...], row >= NPAD, *consts, want_y=False)

    for sub in range(SSD_CHUNKS_PER_STEP):
        rows = slice(sub * CHUNK, (sub + 1) * CHUNK)
        y, _ = _ssd_chunk(actx_ref[rows, :], dtx_ref[rows, :], True, *consts)
        yf_ref[rows, :] = y


def _ssd_bwd_body(act_ref, dt_ref, yf_ref, z_ref, tri_ref, e_ref, biasv_ref, alogv_ref,
                  dskip_ref, nw_ref, out_ref, st_ref):
    @pl.when(pl.program_id(1) == 0)
    def _():
        st_ref[...] = jnp.zeros_like(st_ref)

    for sub in reversed(range(SSD_CHUNKS_PER_STEP)):
        rows = slice(sub * CHUNK, (sub + 1) * CHUNK)
        yb, xs = _ssd_chunk(act_ref[rows, :], dt_ref[rows, :], True, tri_ref[...], e_ref[...],
                            biasv_ref[...], alogv_ref[...], H_SSM, st_ref, False)
        y = yf_ref[rows, :] + yb + xs * dskip_ref[...]
        out_ref[rows, :] = _rms(y * _silu(z_ref[rows, :]), nw_ref[...]).astype(BF16)


def _ssd(act, meta_act, dt_raw, dt_meta, z, consts, nb, seq):
    step = SSD_CHUNKS_PER_STEP * CHUNK
    ns = seq // step
    tril, triu, e_f, e_b, biasv, alogv, dskip_x, ssm_nw = consts
    st_shape = pltpu.VMEM((SSM_G, SSM_N, (H_SSM // SSM_G) * SSM_HD), F32)

    xblk = lambda b, c: (b * ns + c, 0)
    y_f = pl.pallas_call(
        _ssd_fwd_body,
        out_shape=jax.ShapeDtypeStruct((nb * seq, D_SSM), F32),
        grid=(nb, ns),
        in_specs=[
            pl.BlockSpec((step, CONV_DIM), xblk),
            pl.BlockSpec((1, CHUNK, CONV_DIM), lambda b, c: (b, 0, 0)),
            pl.BlockSpec((step, LANES), xblk),
            _const_spec((CHUNK, LANES)),
            _const_spec((CHUNK, CHUNK)), _const_spec((LANES, D_SSM)),
            _const_spec((1, LANES)), _const_spec((1, LANES)),
        ],
        out_specs=pl.BlockSpec((step, D_SSM), xblk),
        scratch_shapes=[st_shape],
        compiler_params=_params(("arbitrary", "arbitrary")), name="ssd_fwd",
    )(act, meta_act, dt_raw, dt_meta, tril, e_f, biasv, alogv)

    rblk = lambda b, c: (b * ns + ns - 1 - c, 0)
    return pl.pallas_call(
        _ssd_bwd_body,
        out_shape=jax.ShapeDtypeStruct((nb * seq, D_SSM), BF16),
        grid=(nb, ns),
        in_specs=[
            pl.BlockSpec((step, CONV_DIM), rblk),
            pl.BlockSpec((step, LANES), rblk),
            pl.BlockSpec((step, D_SSM), rblk),
            pl.BlockSpec((step, D_SSM), rblk),
            _const_spec((CHUNK, CHUNK)), _const_spec((LANES, D_SSM)),
            _const_spec((1, LANES)), _const_spec((1, LANES)),
            _const_spec((1, D_SSM)), _const_spec((1, D_SSM)),
        ],
        out_specs=pl.BlockSpec((step, D_SSM), rblk),
        scratch_shapes=[st_shape],
        compiler_params=_params(("arbitrary", "arbitrary")), name="ssd_bwd",
    )(act, dt_raw, y_f, z, triu, e_b, biasv, alogv, dskip_x, ssm_nw)


def _attn_body(qT_ref, k_ref, vT_ref, km_ref, vmT_ref, slope_ref, lq1_ref, lk1_ref, lq2_ref,
               lk2_ref, nw_ref, out_ref, qaug_ref, sa_ref, sb_ref, acc_ref, dbias_ref,
               *, t, seq, unrolled):
    nkv = seq // t
    slope = slope_ref[0]
    slope_q = slope[:, 0:1]
    row = lax.broadcasted_iota(jnp.int32, (HEAD_W, 1), 0)
    frow = lax.broadcasted_iota(jnp.int32, (HEAD_W, t), 0)
    fcol = lax.broadcasted_iota(jnp.int32, (HEAD_W, t), 1)
    c_hi = (fcol // 256) * 256
    feat = jnp.where(frow < 2, 1.0,
                     jnp.where(frow == 2, -slope_q * c_hi.astype(F32),
                               jnp.where(frow == 3, -slope_q * (fcol - c_hi).astype(F32), 0.0)))
    feats = (feat.astype(BF16), (-feat).astype(BF16))

    first_of_head = pl.program_id(1) == 0
    if not unrolled:
        first_of_head = jnp.logical_and(first_of_head, pl.program_id(2) == 0)

    @pl.when(first_of_head)
    def _():
        d = (lax.broadcasted_iota(jnp.int32, (t, t), 0)
             - lax.broadcasted_iota(jnp.int32, (t, t), 1))
        dbias_ref[...] = jnp.where(d > 0, (-2.0 * slope_q) * d.astype(F32), 0.0)

    def set_queries(qT, slot):
        zero = jnp.zeros_like(qT)
        qz = (jnp.where(row < ATT_DH, qT, zero), jnp.where(row >= ATT_DH, qT, zero))
        for sg in range(2):
            for mp in range(2):
                qaug_ref[slot, sg, mp] = jnp.concatenate([qz[mp], feats[sg]], axis=0)

    def key_feat(rows, rel, valid):
        lane = lax.broadcasted_iota(jnp.int32, (rows, LANES), 1)
        r_hi = (rel // 256) * 256
        f = jnp.where(lane == 0, slope * r_hi.astype(F32),
                      jnp.where(lane == 1, slope * (rel - r_hi).astype(F32),
                                jnp.where(lane < 4, 1.0, 0.0)))
        if valid is not None:
            f = jnp.where(jnp.logical_and(lane == 1, jnp.logical_not(valid)), NEG, f)
        return f.astype(BF16)

    kfeat = key_feat(t, lax.broadcasted_iota(jnp.int32, (t, LANES), 0), None)
    ones_rows = jnp.ones((ONES_ROWS, t), BF16)

    mrow = lax.broadcasted_iota(jnp.int32, (CHUNK, LANES), 0)
    kfeat_m = key_feat(CHUNK, mrow - CHUNK, mrow >= NPAD)
    ones_m = jnp.ones((ONES_ROWS, CHUNK), BF16)
    lam = (jnp.exp(jnp.sum(lq1_ref[...] * lk1_ref[...], axis=-1, keepdims=True))
           - jnp.exp(jnp.sum(lq2_ref[...] * lk2_ref[...], axis=-1, keepdims=True)) + LAM_INIT)
    slots = (sa_ref, sb_ref)

    def as_f32(x):
        return float(x) if isinstance(x, int) else x.astype(F32)

    def x_lhs(j):
        k0 = j * t if isinstance(j, int) else pl.multiple_of(j * t, t)
        return jnp.concatenate([k_ref[0, pl.ds(k0, t), :], kfeat], axis=1)

    def unit_shift(unit, i):
        if unit[0] == "x":
            d = unit[1] - i
            return slope_q * as_f32(-(abs(d) if isinstance(d, int) else jnp.abs(d)) * t)
        return slope_q * as_f32(-(i * t)) if unit[0] == "meta" else 0.0

    def unit_rows(unit):
        return CHUNK if unit[0] == "meta" else t

    def produce(unit, i, qslot, s_ref):
        if unit[0] == "x":
            j = unit[1]
            after = j > i
            sg = int(after) if isinstance(after, bool) else after.astype(jnp.int32)
            lhs = x_lhs(j)
            blocks = [_dot(lhs, qaug_ref[qslot, sg, mp]) for mp in range(2)]
        elif unit[0] == "meta":
            lhs = jnp.concatenate([km_ref[0], kfeat_m], axis=1)
            blocks = [_dot(lhs, qaug_ref[qslot, 0, mp]) for mp in range(2)]
        else:
            lhs = x_lhs(i)
            blocks = [_dot(lhs, qaug_ref[qslot, 0, mp]) + dbias_ref[...] for mp in range(2)]
        rows = unit_rows(unit)
        mbs = []
        for mp, s in enumerate(blocks):
            s_ref[mp, 0:rows, :] = s
            mbs.append(jnp.max(s, axis=0, keepdims=True))
        return tuple(mbs)

    def consume(unit, i, s_ref, mbs, ms):
        cterm = unit_shift(unit, i)
        rows = unit_rows(unit)
        if unit[0] == "meta":
            v_aug = jnp.concatenate([vmT_ref[0, 0], ones_m], axis=0)
        else:
            v_aug = jnp.concatenate([vT_ref[0, unit[1] if unit[0] == "x" else i], ones_rows], axis=0)
        out = []
        for mp in range(2):
            m_new = jnp.maximum(ms[mp], mbs[mp] + cterm)
            p = jnp.exp(s_ref[mp, 0:rows, :] - (m_new - cterm)).astype(BF16)
            alpha = jnp.exp(ms[mp] - m_new)
            acc_ref[mp] = acc_ref[mp] * alpha + _dot(v_aug, p)
            out.append(m_new)
        return tuple(out)

    def start_tile():
        acc_ref[...] = jnp.zeros_like(acc_ref)
        return (jnp.full((1, t), NEG, F32),) * 2

    def finalize(rows):
        o0 = acc_ref[0, 0:HEAD_W, :] / acc_ref[0, HEAD_W:HEAD_W + 1, :]
        o1 = acc_ref[1, 0:HEAD_W, :] / acc_ref[1, HEAD_W:HEAD_W + 1, :]
        o = o0 - lam * o1
        on = o * lax.rsqrt(jnp.mean(o * o, axis=0, keepdims=True) + EPS)
        out_ref[rows, :] = ((on.T * nw_ref[...]) * (1.0 - LAM_INIT)).astype(BF16)

    def run_static(units, pending, ms):
        for u, (i, unit) in enumerate(units):
            nxt = None
            if u + 1 < len(units):
                ni, nunit = units[u + 1]
                if unrolled and ni != i:
                    set_queries(qT_ref[0, :, ni * t:(ni + 1) * t], ni % 2)
                nxt = produce(nunit, ni, ni % 2 if unrolled else 0, slots[(u + 1) % 2])
            if unrolled and (u == 0 or units[u - 1][0] != i):
                ms = start_tile()
            ms = consume(unit, i, slots[u % 2], pending, ms)
            pending = nxt
            if unrolled and (u + 1 == len(units) or units[u + 1][0] != i):
                finalize(slice(i * t, (i + 1) * t))
        return ms

    tail = [("meta",), ("diag",)]
    if unrolled:
        units = []
        for i in range(nkv):
            units += [(i, ("x", n + (n >= i))) for n in range(nkv - 1)] + [(i, u) for u in tail]
        set_queries(qT_ref[0, :, 0:t], 0)
        run_static(units, produce(units[0][1], 0, 0, sa_ref), None)
    else:
        i = pl.program_id(2)
        set_queries(qT_ref[0], 0)
        ms = start_tile()
        n_off = nkv - 1
        xunit = lambda n: ("x", n + (n >= i).astype(jnp.int32))
        rest = tail
        if n_off >= 1:
            mbs = produce(xunit(0), i, 0, sa_ref)
            pairs = (n_off - 1) // 2

            def pair_body(u, carry):
                ms, mbs = carry[:2], carry[2:]
                n = 2 * u
                nxt = produce(xunit(n + 1), i, 0, sb_ref)
                ms = consume(xunit(n), i, sa_ref, mbs, ms)
                mbs = produce(xunit(n + 2), i, 0, sa_ref)
                ms = consume(xunit(n + 1), i, sb_ref, nxt, ms)
                return ms + mbs

            unroll = next((f for f in (5, 3) if pairs % f == 0), 1)
            carry = lax.fori_loop(0, pairs, pair_body, ms + mbs, unroll=unroll)
            ms, mbs = carry[:2], carry[2:]
            rest = [xunit(n) for n in range(2 * pairs, n_off)] + tail
        else:
            mbs = produce(rest[0], i, 0, sa_ref)
        run_static([(i, u) for u in rest], mbs, ms)
        finalize(slice(None))


def _attention(qT, k, vT, k_meta, vT_meta, slopes, lq1, lk1, lq2, lk2, attn_nw, nb, seq):
    t = min(ROW_TILE, seq)
    nq = seq // t
    unrolled = nq * (nq + 1) <= ATTN_UNROLL_MAX_UNITS
    if unrolled:
        grid = (N_HEADS, nb)
        tq, q_map, out_map = seq, (lambda h, b: (h, 0, b)), (lambda h, b: (b, h))
        per_seq = lambda f: (lambda h, b: f(h, b))
    else:
        grid = (N_HEADS, nb, nq)
        tq = t
        q_map, out_map = (lambda h, b, i: (h, 0, b * nq + i)), (lambda h, b, i: (b * nq + i, h))
        per_seq = lambda f: (lambda h, b, i: f(h, b))
    vec = lambda n: _const_spec((1, n))
    in_specs = [
        pl.BlockSpec((1, HEAD_W, tq), q_map),
        pl.BlockSpec((1, seq, HEAD_W), per_seq(lambda h, b: (h, b, 0))),
        pl.BlockSpec((1, nq, HEAD_W, t), per_seq(lambda h, b: (h, b, 0, 0))),
        pl.BlockSpec((1, CHUNK, HEAD_W), per_seq(lambda h, b: (h, 0, 0))),
        pl.BlockSpec((1, 1, HEAD_W, CHUNK), per_seq(lambda h, b: (h, 0, 0, 0))),
        pl.BlockSpec((1, 1, LANES), per_seq(lambda h, b: (h, 0, 0))),
        vec(ATT_DH), vec(ATT_DH), vec(ATT_DH), vec(ATT_DH), vec(HEAD_W),
    ]
    return pl.pallas_call(
        functools.partial(_attn_body, t=t, seq=seq, unrolled=unrolled),
        out_shape=jax.ShapeDtypeStruct((nb * seq, ATT_VDIM), BF16),
        grid=grid, in_specs=in_specs,
        out_specs=pl.BlockSpec((tq, HEAD_W), out_map),
        scratch_shapes=[pltpu.VMEM((2, 2, 2, 2 * HEAD_W, t), BF16),
                        pltpu.VMEM((2, t, t), F32), pltpu.VMEM((2, t, t), F32),
                        pltpu.VMEM((2, HEAD_W + ONES_ROWS, t), F32),
                        pltpu.VMEM((t, t), F32)],
        compiler_params=_params(("arbitrary",) * len(grid)), name="attn",
    )(qT, k, vT, k_meta, vT_meta, slopes, lq1, lk1, lq2, lk2, attn_nw)


def _mlp_body(x_ref, att_ref, ssm_ref, wo_ref, n2_ref, wup_ref, wdown_ref, fw_ref, out_ref):
    h1 = (x_ref[...] + _dot(att_ref[...], wo_ref[0:ATT_VDIM, :])
          + _dot(ssm_ref[...], wo_ref[ATT_VDIM:ATT_VDIM + D_SSM, :]))
    u2 = _rms(h1, n2_ref[...]).astype(BF16)
    hid = jnp.maximum(_dot(u2, wup_ref[...]), 0.0)
    h2 = h1 + _dot((hid * hid).astype(BF16), wdown_ref[...])
    out_ref[...] = _rms(h2, fw_ref[...])


def _mlp(x2d, att, ssm, wo, n2, wup, wdown, fw, tm):
    rows = x2d.shape[0]
    tile = lambda w: pl.BlockSpec((tm, w), lambda i: (i, 0))
    return pl.pallas_call(
        _mlp_body, out_shape=jax.ShapeDtypeStruct((rows, D_MODEL), F32), grid=(rows // tm,),
        in_specs=[tile(D_MODEL), tile(ATT_VDIM), tile(D_SSM), _const_spec(wo.shape),
                  _const_spec((1, D_MODEL)), _const_spec(wup.shape), _const_spec(wdown.shape),
                  _const_spec((1, D_MODEL))],
        out_specs=tile(D_MODEL),
        compiler_params=_params(("arbitrary",)), name="mlp",
    )(x2d, att, ssm, wo, n2, wup, wdown, fw)


def _head_lane_vec(fwd_vals, bwd_vals):
    v = jnp.zeros((1, LANES), F32)
    v = v.at[0, 0:H_SSM].set(fwd_vals.astype(F32))
    return v.at[0, H_SSM:2 * H_SSM].set(bwd_vals.astype(F32))


def _expand_matrix(lane_lo):
    lane = jnp.arange(LANES)[:, None]
    ch = jnp.arange(D_SSM)[None, :] // SSM_HD
    return (lane == ch + lane_lo).astype(BF16)


def _trunk(x, meta_parts, p):
    nb, seq, _ = x.shape
    x2d = x.reshape(nb * seq, D_MODEL)
    k_meta, vT_meta, xbc_meta, dt_meta = meta_parts
    tm = min(ROW_TILE, seq)
    qT, k, vT, z, act, dt_raw, meta_act = _inproj(
        x2d, xbc_meta, p["n1"], p["w_in"], p["conv_w8"], p["conv_b"], nb, seq, tm)
    ssm = _ssd(act, meta_act, dt_raw, dt_meta, z, p["ssd_consts"], nb, seq)
    att = _attention(qT, k, vT, k_meta, vT_meta, p["slopes"], p["lq1"], p["lk1"], p["lq2"],
                     p["lk2"], p["attn_nw"], nb, seq)
    y = _mlp(x2d, att, ssm, p["wo"], p["n2"], p["wup"], p["wdown"], p["fw"], tm)
    return y.reshape(nb, seq, D_MODEL)


@jax.jit
def kernel(x_prompt, x_sample, meta_tokens, norm1_w, w_in, conv_w, conv_b, lambda_q1, lambda_k1,
           lambda_q2, lambda_k2, attn_norm_w, dt_bias_f, dt_bias_b, a_log_f, a_log_b, d_skip,
           ssm_norm_w, w_out, norm2_w, w_up, w_down, final_norm_w):
    li = 0
    w = w_in[li].astype(BF16)
    cuts = (0, ATT_VDIM, 2 * ATT_VDIM, 3 * ATT_VDIM, 3 * ATT_VDIM + D_SSM,
            3 * ATT_VDIM + D_SSM + CONV_DIM)
    wdt = jnp.zeros((D_MODEL, LANES), BF16).at[:, 0:2 * H_SSM].set(w[:, cuts[5]:])
    t = jnp.arange(CHUNK)
    p = {
        "n1": norm1_w[li].reshape(1, D_MODEL).astype(F32),
        "w_in": tuple(w[:, cuts[j]:cuts[j + 1]] for j in range(5)) + (wdt,),
        "conv_w8": jnp.zeros((HALO, CONV_DIM), F32).at[0:D_CONV].set(conv_w[li].astype(F32)),
        "conv_b": conv_b[li].reshape(1, CONV_DIM).astype(F32),
        "ssd_consts": (
            (t[:, None] >= t[None, :]).astype(BF16),
            (t[None, :] >= t[:, None]).astype(BF16),
            _expand_matrix(0), _expand_matrix(H_SSM),
            _head_lane_vec(dt_bias_f[li], dt_bias_b[li]),
            _head_lane_vec(a_log_f[li], a_log_b[li]),
            jnp.repeat(d_skip[li].astype(F32), SSM_HD).reshape(1, D_SSM),
            ssm_norm_w[li].reshape(1, D_SSM).astype(F32),
        ),
        "slopes": jnp.broadcast_to(
            (2.0 ** (-8.0 * (jnp.arange(N_HEADS, dtype=F32) + 1.0) / N_HEADS))[:, None, None],
            (N_HEADS, 1, LANES)),
        "lq1": lambda_q1[li].reshape(1, ATT_DH).astype(F32),
        "lk1": lambda_k1[li].reshape(1, ATT_DH).astype(F32),
        "lq2": lambda_q2[li].reshape(1, ATT_DH).astype(F32),
        "lk2": lambda_k2[li].reshape(1, ATT_DH).astype(F32),
        "attn_nw": attn_norm_w[li].reshape(1, HEAD_W).astype(F32),
        "wo": w_out[li].astype(BF16),
        "n2": norm2_w[li].reshape(1, D_MODEL).astype(F32),
        "wup": w_up[li].astype(BF16),
        "wdown": w_down[li].astype(BF16),
        "fw": final_norm_w.reshape(1, D_MODEL).astype(F32),
    }
    meta_block = jnp.concatenate(
        [jnp.zeros((NPAD, D_MODEL), F32), meta_tokens.astype(F32)], axis=0)
    _, k_m, vT_m, _, xbc_m, dt_m = _inproj_meta(meta_block, p["n1"], p["w_in"])
    meta_parts = (k_m, vT_m, xbc_m, dt_m)
    return (_trunk(x_prompt, meta_parts, p), _trunk(x_sample, meta_parts, p))
```

```python
import functools
import math

import jax
import jax.numpy as jnp
import numpy as np
from jax import lax
from jax.experimental import pallas as pl
from jax.experimental.pallas import tpu as pltpu

F32 = jnp.float32
BF16 = jnp.bfloat16

D_MODEL = 1024
N_META = 16
CHUNK = 128
NPAD = CHUNK - N_META
N_HEADS = 8
ATT_DH = 64
HEAD_W = 2 * ATT_DH
ATT_VDIM = N_HEADS * HEAD_W
D_SSM = 1024
H_SSM = 16
SSM_HD = 64
SSM_G = 2
SSM_N = 128
D_CONV = 5
CONV_DIM = D_SSM + 2 * SSM_G * SSM_N
D_FF = 4 * D_MODEL
EPS = 1e-5
LAM_INIT = 0.8 - 0.6 * math.exp(-0.3 * 0)
NEG = -1e30
LOG2E = math.log2(math.e)


def _bf16_parts(x, n):
    parts, r = [], np.float32(x)
    for _ in range(n):
        h = np.float32(np.asarray(r, dtype=jnp.bfloat16))
        parts.append(float(h))
        r = np.float32(r - h)
    return tuple(parts)


LOG2E_PARTS = _bf16_parts(LOG2E, 3)

LANES = 128
HALO = 8
VMEM_LIMIT = 56 * 1024 * 1024

ROW_TILE = 512
ONES_ROWS = 16
SSD_CHUNKS_PER_STEP = 2
ATTN_UNROLL_MAX_UNITS = 24


def _dot(a, b):
    return jnp.dot(a, b, preferred_element_type=F32)


def _rms(x, w):
    return x * lax.rsqrt(jnp.mean(x * x, axis=-1, keepdims=True) + EPS) * w


def _silu(x):
    return x / (1.0 + jnp.exp(-x))


def _const_spec(shape):
    nd = len(shape)
    return pl.BlockSpec(shape, lambda *_: (0,) * nd)


def _params(sem, flags=None):
    return pltpu.CompilerParams(dimension_semantics=sem, vmem_limit_bytes=VMEM_LIMIT, flags=flags)


def _conv_taps(ext, w, b, rows):
    n = ext.shape[0]
    acc = b
    for j in range(D_CONV):
        shifted = ext if j == 2 else pltpu.roll(ext, (2 - j) % n, 0)
        acc = acc + w[j:j + 1, :] * shifted[HALO:HALO + rows, :]
    return _silu(acc)


def _project(u, wq_ref, wk_ref, wv_ref, wz_ref, wdt_ref, qT_ref, k_ref, vT_ref, z_ref, dt_ref):
    q = _dot(u, wq_ref[...]) * (ATT_DH ** -0.5 * LOG2E)
    for h in range(N_HEADS):
        qT_ref[h] = q[:, h * HEAD_W:(h + 1) * HEAD_W].T.astype(BF16)
    k = _dot(u, wk_ref[...])
    for h in range(N_HEADS):
        k_ref[h] = k[:, h * HEAD_W:(h + 1) * HEAD_W].astype(BF16)
    v = _dot(u, wv_ref[...])
    for h in range(N_HEADS):
        vT_ref[h, 0] = v[:, h * HEAD_W:(h + 1) * HEAD_W].T.astype(BF16)
    z_ref[...] = _dot(u, wz_ref[...])
    dt_ref[...] = _dot(u, wdt_ref[...])


def _inproj_meta_body(x_ref, n1_ref, wq_ref, wk_ref, wv_ref, wz_ref, wx_ref, wdt_ref,
                      qT_ref, k_ref, vT_ref, z_ref, xbc_ref, dt_ref):
    u = _rms(x_ref[...], n1_ref[...]).astype(BF16)
    _project(u, wq_ref, wk_ref, wv_ref, wz_ref, wdt_ref, qT_ref, k_ref, vT_ref, z_ref, dt_ref)
    xbc_ref[...] = _dot(u, wx_ref[...])


def _inproj_body(x_ref, xprev_ref, xnext_ref, metaraw_ref, n1_ref, wq_ref, wk_ref, wv_ref, wz_ref,
                 wx_ref, wdt_ref, cw_ref, cb_ref,
                 qT_ref, k_ref, vT_ref, z_ref, act_ref, dt_ref, metaact_ref, *, tm, nt):
    i = pl.program_id(1)
    n1 = n1_ref[...]
    x_ext = jnp.concatenate([xprev_ref[...], x_ref[...], xnext_ref[...]], axis=0)
    u_ext = _rms(x_ext, n1).astype(BF16)
    raw = _dot(u_ext, wx_ref[...])
    w = cw_ref[...]
    b = cb_ref[...]
    prev = jnp.where(i == 0, metaraw_ref[CHUNK - HALO:CHUNK, :], raw[0:HALO, :])
    nxt = jnp.where(i == nt - 1, 0.0, raw[HALO + tm:, :])
    main = raw[HALO:HALO + tm, :]
    act_ref[...] = _conv_taps(jnp.concatenate([prev, main, nxt], axis=0), w, b, tm)
    _project(u_ext[HALO:HALO + tm, :], wq_ref, wk_ref, wv_ref, wz_ref, wdt_ref,
             qT_ref, k_ref, vT_ref, z_ref, dt_ref)

    @pl.when(i == 0)
    def _():
        zeros = jnp.zeros((HALO, CONV_DIM), F32)
        extm = jnp.concatenate([zeros, metaraw_ref[...], main[0:HALO, :]], axis=0)
        am = _conv_taps(extm, w, b, CHUNK)
        row = lax.broadcasted_iota(jnp.int32, (CHUNK, 1), 0)
        metaact_ref[0] = jnp.where(row >= NPAD, am, 0.0)


def _inproj_specs(rows, tm, tile_map):
    out_shape = (
        jax.ShapeDtypeStruct((N_HEADS, HEAD_W, rows), BF16),
        jax.ShapeDtypeStruct((N_HEADS, rows, HEAD_W), BF16),
        jax.ShapeDtypeStruct((N_HEADS, rows // tm, HEAD_W, tm), BF16),
        jax.ShapeDtypeStruct((rows, D_SSM), F32),
        jax.ShapeDtypeStruct((rows, CONV_DIM), F32),
        jax.ShapeDtypeStruct((rows, LANES), F32),
    )
    at = lambda f: (lambda *g: f(tile_map(*g)))
    out_specs = (
        pl.BlockSpec((N_HEADS, HEAD_W, tm), at(lambda r: (0, 0, r))),
        pl.BlockSpec((N_HEADS, tm, HEAD_W), at(lambda r: (0, r, 0))),
        pl.BlockSpec((N_HEADS, 1, HEAD_W, tm), at(lambda r: (0, r, 0, 0))),
        pl.BlockSpec((tm, D_SSM), at(lambda r: (r, 0))),
        pl.BlockSpec((tm, CONV_DIM), at(lambda r: (r, 0))),
        pl.BlockSpec((tm, LANES), at(lambda r: (r, 0))),
    )
    return out_shape, out_specs


def _inproj_meta(meta_block, n1, w_in):
    out_shape, out_specs = _inproj_specs(CHUNK, CHUNK, lambda i: i)
    return pl.pallas_call(
        _inproj_meta_body, out_shape=out_shape, grid=(1,),
        in_specs=[pl.BlockSpec((CHUNK, D_MODEL), lambda i: (i, 0)), _const_spec((1, D_MODEL))]
        + [_const_spec(w.shape) for w in w_in],
        out_specs=out_specs, compiler_params=_params(("arbitrary",)), name="inproj_meta",
    )(meta_block, n1, *w_in)


def _inproj(x2d, meta_raw, n1, w_in, conv_w8, conv_b, nb, seq, tm):
    nt = seq // tm
    hb = tm // HALO
    nblk8 = (nb * seq) // HALO
    out_shape, out_specs = _inproj_specs(nb * seq, tm, lambda b, i: b * nt + i)
    out_shape += (jax.ShapeDtypeStruct((nb, CHUNK, CONV_DIM), F32),)
    out_specs += (pl.BlockSpec((1, CHUNK, CONV_DIM), lambda b, i: (b, 0, 0)),)
    in_specs = [
        pl.BlockSpec((tm, D_MODEL), lambda b, i: (b * nt + i, 0)),
        pl.BlockSpec((HALO, D_MODEL), lambda b, i: (jnp.maximum((b * nt + i) * hb - 1, 0), 0)),
        pl.BlockSpec((HALO, D_MODEL),
                     lambda b, i: (jnp.minimum((b * nt + i + 1) * hb, nblk8 - 1), 0)),
        _const_spec((CHUNK, CONV_DIM)), _const_spec((1, D_MODEL)),
    ] + [_const_spec(w.shape) for w in w_in] + [_const_spec((HALO, CONV_DIM)),
                                                 _const_spec((1, CONV_DIM))]
    return pl.pallas_call(
        functools.partial(_inproj_body, tm=tm, nt=nt), out_shape=out_shape, grid=(nb, nt),
        in_specs=in_specs, out_specs=out_specs,
        compiler_params=_params(("arbitrary", "arbitrary")), name="inproj",
    )(x2d, x2d, x2d, meta_raw, n1, *w_in, conv_w8, conv_b)


def _split3(x):
    hi = x.astype(BF16)
    r = x - hi.astype(F32)
    mid = r.astype(BF16)
    lo = (r - mid.astype(F32)).astype(BF16)
    return hi, mid, lo


def _expand_heads(x, e2):
    hi = x.astype(BF16)
    lo = (x - hi.astype(F32)).astype(BF16)
    return _dot(jnp.concatenate([hi, lo], axis=1), e2)


def _ssd_chunk(act, dtr, valid, tri, e, biasv, alogv, lane_lo, st_ref, fwd, want_y=True):
    xs = act[:, :D_SSM]
    lane = lax.broadcasted_iota(jnp.int32, (1, LANES), 1)
    hmask = (lane >= lane_lo) & (lane < lane_lo + H_SSM)
    x = dtr + biasv
    dt = jnp.maximum(x, 0.0) + jnp.log(1.0 + jnp.exp(-jnp.abs(x)))
    dt = jnp.where(hmask & valid, dt, 0.0)
    adt = dt * (-jnp.exp(alogv))
    hi, mid, lo = _split3(adt)
    acs = (_dot(jnp.concatenate([tri, tri], axis=1), jnp.concatenate([hi, mid], axis=0))
           + _dot(tri, lo))
    acs_t = acs.T
    edge = acs[CHUNK - 1:CHUNK, :] if fwd else acs[0:1, :]
    e2 = jnp.concatenate([e, e], axis=0)
    dt_x = _expand_heads(dt, e2)
    ein_x = _expand_heads(jnp.exp(acs), e2)
    est_x = _expand_heads(jnp.exp(edge - acs), e2)
    cd_x = ein_x[CHUNK - 1:CHUNK, :] if fwd else ein_x[0:1, :]
    xg = xs * dt_x
    xgd = (xg * est_x).astype(BF16)
    li = lax.broadcasted_iota(jnp.int32, (CHUNK, CHUNK), 0)
    si = lax.broadcasted_iota(jnp.int32, (CHUNK, CHUNK), 1)
    tmask = (li >= si) if fwd else (si >= li)
    left = lax.broadcasted_iota(jnp.int32, (1, LANES), 1) < SSM_HD
    ys = []
    heads_per_group = H_SSM // SSM_G
    for g in range(SSM_G):
        bg = act[:, D_SSM + g * SSM_N:D_SSM + (g + 1) * SSM_N]
        cg = act[:, D_SSM + SSM_G * SSM_N + g * SSM_N:D_SSM + SSM_G * SSM_N + (g + 1) * SSM_N]
        cb = cg.astype(BF16)
        bb = bg.astype(BF16)
        st = st_ref[g]
        gsl = slice(g * heads_per_group * SSM_HD, (g + 1) * heads_per_group * SSM_HD)
        st_ref[g] = st * cd_x[:, gsl] + _dot(bg.T.astype(BF16), xgd[:, gsl])
        if not want_y:
            continue
        gm = lax.dot_general(cb, bb, (((1,), (1,)), ((), ())), preferred_element_type=F32)
        yoff = _dot(cb, st.astype(BF16))
        for jp in range(heads_per_group // 2):
            pair = g * (heads_per_group // 2) + jp
            sl = slice(pair * LANES, (pair + 1) * LANES)
            xg_pair = xg[:, sl]
            mds = []
            for hh in range(2):
                c = lane_lo + pair * 2 + hh
                seg = acs[:, c:c + 1] - acs_t[c:c + 1, :]
                decay = jnp.exp(jnp.where(tmask, seg, NEG))
                mds.append((gm * decay).astype(BF16))
            xg2 = jnp.concatenate([jnp.where(left, xg_pair, 0.0).astype(BF16),
                                   jnp.where(left, 0.0, xg_pair).astype(BF16)], axis=0)
            ys.append(ein_x[:, sl] * yoff[:, jp * LANES:(jp + 1) * LANES]
                      + _dot(jnp.concatenate(mds, axis=1), xg2))
    return (jnp.concatenate(ys, axis=1) if want_y else None), xs


def _ssd_fwd_body(actx_ref, actm_ref, dtx_ref, dtm_ref, tri_ref, e_ref, biasv_ref, alogv_ref,
                  yf_ref, st_ref):
    consts = (tri_ref[...], e_ref[...], biasv_ref[...], alogv_ref[...], 0, st_ref, True)

    @pl.when(pl.program_id(1) == 0)
    def _():
        st_ref[...] = jnp.zeros_like(st_ref)
        row = lax.broadcasted_iota(jnp.int32, (CHUNK, 1), 0)
        _ssd_chunk(actm_ref[0], dtm_ref[...], row >= NPAD, *consts, want_y=False)

    for sub in range(SSD_CHUNKS_PER_STEP):
        rows = slice(sub * CHUNK, (sub + 1) * CHUNK)
        y, _ = _ssd_chunk(actx_ref[rows, :], dtx_ref[rows, :], True, *consts)
        yf_ref[rows, :] = y


def _ssd_bwd_body(act_ref, dt_ref, yf_ref, z_ref, tri_ref, e_ref, biasv_ref, alogv_ref,
                  dskip_ref, nw_ref, out_ref, st_ref):
    @pl.when(pl.program_id(1) == 0)
    def _():
        st_ref[...] = jnp.zeros_like(st_ref)

    for sub in reversed(range(SSD_CHUNKS_PER_STEP)):
        rows = slice(sub * CHUNK, (sub + 1) * CHUNK)
        yb, xs = _ssd_chunk(act_ref[rows, :], dt_ref[rows, :], True, tri_ref[...], e_ref[...],
                            biasv_ref[...], alogv_ref[...], H_SSM, st_ref, False)
        y = yf_ref[rows, :] + yb + xs * dskip_ref[...]
        out_ref[rows, :] = _rms(y * _silu(z_ref[rows, :]), nw_ref[...]).astype(BF16)


def _ssd(act, meta_act, dt_raw, dt_meta, z, consts, nb, seq):
    step = SSD_CHUNKS_PER_STEP * CHUNK
    ns = seq // step
    tril, triu, e_f, e_b, biasv, alogv, dskip_x, ssm_nw = consts
    st_shape = pltpu.VMEM((SSM_G, SSM_N, (H_SSM // SSM_G) * SSM_HD), F32)

    xblk = lambda b, c: (b * ns + c, 0)
    y_f = pl.pallas_call(
        _ssd_fwd_body,
        out_shape=jax.ShapeDtypeStruct((nb * seq, D_SSM), F32),
        grid=(nb, ns),
        in_specs=[
            pl.BlockSpec((step, CONV_DIM), xblk),
            pl.BlockSpec((1, CHUNK, CONV_DIM), lambda b, c: (b, 0, 0)),
            pl.BlockSpec((step, LANES), xblk),
            _const_spec((CHUNK, LANES)),
            _const_spec((CHUNK, CHUNK)), _const_spec((LANES, D_SSM)),
            _const_spec((1, LANES)), _const_spec((1, LANES)),
        ],
        out_specs=pl.BlockSpec((step, D_SSM), xblk),
        scratch_shapes=[st_shape],
        compiler_params=_params(("arbitrary", "arbitrary")), name="ssd_fwd",
    )(act, meta_act, dt_raw, dt_meta, tril, e_f, biasv, alogv)

    rblk = lambda b, c: (b * ns + ns - 1 - c, 0)
    return pl.pallas_call(
        _ssd_bwd_body,
        out_shape=jax.ShapeDtypeStruct((nb * seq, D_SSM), BF16),
        grid=(nb, ns),
        in_specs=[
            pl.BlockSpec((step, CONV_DIM), rblk),
            pl.BlockSpec((step, LANES), rblk),
            pl.BlockSpec((step, D_SSM), rblk),
            pl.BlockSpec((step, D_SSM), rblk),
            _const_spec((CHUNK, CHUNK)), _const_spec((LANES, D_SSM)),
            _const_spec((1, LANES)), _const_spec((1, LANES)),
            _const_spec((1, D_SSM)), _const_spec((1, D_SSM)),
        ],
        out_specs=pl.BlockSpec((step, D_SSM), rblk),
        scratch_shapes=[st_shape],
        compiler_params=_params(("arbitrary", "arbitrary")), name="ssd_bwd",
    )(act, dt_raw, y_f, z, triu, e_b, biasv, alogv, dskip_x, ssm_nw)


def _attn_body(qT_ref, k_ref, vT_ref, km_ref, vmT_ref, slope_ref, lq1_ref, lk1_ref, lq2_ref,
               lk2_ref, nw_ref, out_ref, qaug_ref, sa_ref, sb_ref, acc_ref, dbias_ref,
               *, t, seq, unrolled):
    nkv = seq // t
    slope = slope_ref[0]
    slope_q = slope[:, 0:1]
    row = lax.broadcasted_iota(jnp.int32, (HEAD_W, 1), 0)
    frow = lax.broadcasted_iota(jnp.int32, (HEAD_W, t), 0)
    fcol = lax.broadcasted_iota(jnp.int32, (HEAD_W, t), 1)
    c_hi = (fcol // 256) * 256
    def log2e_part(idx):
        part = idx % 3
        return jnp.where(part == 0, LOG2E_PARTS[0],
                         jnp.where(part == 1, LOG2E_PARTS[1], LOG2E_PARTS[2]))

    feat = jnp.where(frow < 6, log2e_part(frow),
                     jnp.where(frow < 9, -slope_q * c_hi.astype(F32),
                               jnp.where(frow < 12, -slope_q * (fcol - c_hi).astype(F32), 0.0)))
    feats = (feat.astype(BF16), (-feat).astype(BF16))

    first_of_head = jnp.logical_and(pl.program_id(1) == 0, pl.program_id(2) == 0)

    @pl.when(first_of_head)
    def _():
        d = (lax.broadcasted_iota(jnp.int32, (t, t), 0)
             - lax.broadcasted_iota(jnp.int32, (t, t), 1))
        dbias_ref[...] = jnp.where(d > 0, (-2.0 * LOG2E * slope_q) * d.astype(F32), 0.0)

    def set_queries(qT, slot):
        zero = jnp.zeros_like(qT)
        qz = (jnp.where(row < ATT_DH, qT, zero), jnp.where(row >= ATT_DH, qT, zero))
        for sg in range(2):
            for mp in range(2):
                qaug_ref[slot, sg, mp] = jnp.concatenate([qz[mp], feats[sg]], axis=0)

    def key_feat(rows, rel, valid):
        lane = lax.broadcasted_iota(jnp.int32, (rows, LANES), 1)
        r_hi = (rel // 256) * 256
        f = jnp.where(lane < 3, slope * r_hi.astype(F32),
                      jnp.where(lane < 6, slope * (rel - r_hi).astype(F32),
                                jnp.where(lane < 12, log2e_part(lane), 0.0)))
        if valid is not None:
            f = jnp.where(valid, f, jnp.where(lane == 3, NEG, 0.0))
        return f.astype(BF16)

    kfeat = key_feat(t, lax.broadcasted_iota(jnp.int32, (t, LANES), 0), None)
    ones_rows = jnp.ones((ONES_ROWS, t), BF16)

    mrow = lax.broadcasted_iota(jnp.int32, (CHUNK, LANES), 0)
    kfeat_m = key_feat(CHUNK, mrow - CHUNK, mrow >= NPAD)
    ones_m = jnp.ones((ONES_ROWS, CHUNK), BF16)
    lam = (jnp.exp(jnp.sum(lq1_ref[...] * lk1_ref[...], axis=-1, keepdims=True))
           - jnp.exp(jnp.sum(lq2_ref[...] * lk2_ref[...], axis=-1, keepdims=True)) + LAM_INIT)
    slots = (sa_ref, sb_ref)

    def as_f32(x):
        return float(x) if isinstance(x, int) else x.astype(F32)

    def x_lhs(j):
        k0 = j * t if isinstance(j, int) else pl.multiple_of(j * t, t)
        return jnp.concatenate([k_ref[0, pl.ds(k0, t), :], kfeat], axis=1)

    def unit_shift(unit, i):
        if unit[0] == "x":
            d = unit[1] - i
            return (LOG2E * slope_q) * as_f32(-(abs(d) if isinstance(d, int) else jnp.abs(d)) * t)
        return (LOG2E * slope_q) * as_f32(-(i * t)) if unit[0] == "meta" else 0.0

    def unit_rows(unit):
        return CHUNK if unit[0] == "meta" else t

    def produce(unit, i, qslot, s_ref):
        if unit[0] == "x":
            j = unit[1]
            after = j > i
            sg = int(after) if isinstance(after, bool) else after.astype(jnp.int32)
            lhs = x_lhs(j)
            blocks = [_dot(lhs, qaug_ref[qslot, sg, mp]) for mp in range(2)]
        elif unit[0] == "meta":
            lhs = jnp.concatenate([km_ref[0], kfeat_m], axis=1)
            blocks = [_dot(lhs, qaug_ref[qslot, 0, mp]) for mp in range(2)]
        else:
            lhs = x_lhs(i)
            blocks = [_dot(lhs, qaug_ref[qslot, 0, mp]) + dbias_ref[...] for mp in range(2)]
        rows = unit_rows(unit)
        mbs = []
        for mp, s in enumerate(blocks):
            s_ref[mp, 0:rows, :] = s
            mbs.append(jnp.max(s, axis=0, keepdims=True))
        return tuple(mbs)

    def consume(unit, i, s_ref, mbs, ms):
        cterm = unit_shift(unit, i)
        rows = unit_rows(unit)
        if unit[0] == "meta":
            v_aug = jnp.concatenate([vmT_ref[0, 0], ones_m], axis=0)
        else:
            v_aug = jnp.concatenate([vT_ref[0, unit[1] if unit[0] == "x" else i], ones_rows], axis=0)
        out = []
        for mp in range(2):
            m_new = jnp.maximum(ms[mp], mbs[mp] + cterm)
            p = jnp.exp2(s_ref[mp, 0:rows, :] - (m_new - cterm)).astype(BF16)
            alpha = jnp.exp2(ms[mp] - m_new)
            acc_ref[mp] = acc_ref[mp] * alpha + _dot(v_aug, p)
            out.append(m_new)
        return tuple(out)

    def start_tile():
        acc_ref[...] = jnp.zeros_like(acc_ref)
        return (jnp.full((1, t), NEG, F32),) * 2

    def finalize(rows):
        o0 = acc_ref[0, 0:HEAD_W, :] / acc_ref[0, HEAD_W:HEAD_W + 1, :]
        o1 = acc_ref[1, 0:HEAD_W, :] / acc_ref[1, HEAD_W:HEAD_W + 1, :]
        o = o0 - lam * o1
        on = o * lax.rsqrt(jnp.mean(o * o, axis=0, keepdims=True) + EPS)
        out_ref[rows, :] = ((on.T * nw_ref[...]) * (1.0 - LAM_INIT)).astype(BF16)

    def run_static(units, pending, ms):
        for u, (l, i, unit) in enumerate(units):
            nxt = None
            if u + 1 < len(units):
                nl, ni, nunit = units[u + 1]
                if nl != l:
                    set_queries(qT_ref[0, :, nl * t:(nl + 1) * t], nl % 2)
                nxt = produce(nunit, ni, nl % 2, slots[(u + 1) % 2])
            if unrolled and (u == 0 or units[u - 1][0] != l):
                ms = start_tile()
            ms = consume(unit, i, slots[u % 2], pending, ms)
            pending = nxt
            if unrolled and (u + 1 == len(units) or units[u + 1][0] != l):
                finalize(slice(l * t, (l + 1) * t))
        return ms

    def x_units(i):
        return [("x", n + (n >= i)) for n in range(nkv - 1)]

    tail = [("diag",), ("meta",)]
    if unrolled:
        units = []
        for l in range(unrolled):
            units += [(l, l, u) for u in x_units(l) + tail]
        set_queries(qT_ref[0, :, 0:t], 0)
        run_static(units, produce(units[0][2], 0, 0, sa_ref), None)
    else:
        i = pl.program_id(2)
        set_queries(qT_ref[0], 0)
        ms = start_tile()
        n_off = nkv - 1
        xunit = lambda n: ("x", n + (n >= i).astype(jnp.int32))
        rest = tail
        if n_off >= 1:
            mbs = produce(xunit(0), i, 0, sa_ref)
            pairs = (n_off - 1) // 2

            def pair_body(u, carry):
                ms, mbs = carry[:2], carry[2:]
                n = 2 * u
                nxt = produce(xunit(n + 1), i, 0, sb_ref)
                ms = consume(xunit(n), i, sa_ref, mbs, ms)
                mbs = produce(xunit(n + 2), i, 0, sa_ref)
                ms = consume(xunit(n + 1), i, sb_ref, nxt, ms)
                return ms + mbs

            unroll = next((f for f in (5, 3) if pairs % f == 0), 1)
            carry = lax.fori_loop(0, pairs, pair_body, ms + mbs, unroll=unroll)
            ms, mbs = carry[:2], carry[2:]
            rest = [xunit(n) for n in range(2 * pairs, n_off)] + tail
        else:
            mbs = produce(rest[0], i, 0, sa_ref)
        run_static([(0, i, u) for u in rest], mbs, ms)
        finalize(slice(None))


def _attention(qT, k, vT, k_meta, vT_meta, slopes, lq1, lk1, lq2, lk2, attn_nw, nb, seq):
    t = min(ROW_TILE, seq)
    nq = seq // t
    unrolled = nq if nq * (nq + 1) <= ATTN_UNROLL_MAX_UNITS else 0
    tq = max(unrolled, 1) * t
    steps = seq // tq
    grid = (N_HEADS, nb, steps)
    q_map = lambda h, b, i: (h, 0, b * steps + i)
    out_map = lambda h, b, i: (b * steps + i, h)
    per_seq = lambda f: (lambda h, b, i: f(h, b))
    vec = lambda n: _const_spec((1, n))
    in_specs = [
        pl.BlockSpec((1, HEAD_W, tq), q_map),
        pl.BlockSpec((1, seq, HEAD_W), per_seq(lambda h, b: (h, b, 0))),
        pl.BlockSpec((1, nq, HEAD_W, t), per_seq(lambda h, b: (h, b, 0, 0))),
        pl.BlockSpec((1, CHUNK, HEAD_W), per_seq(lambda h, b: (h, 0, 0))),
        pl.BlockSpec((1, 1, HEAD_W, CHUNK), per_seq(lambda h, b: (h, 0, 0, 0))),
        pl.BlockSpec((1, 1, LANES), per_seq(lambda h, b: (h, 0, 0))),
        vec(ATT_DH), vec(ATT_DH), vec(ATT_DH), vec(ATT_DH), vec(HEAD_W),
    ]
    return pl.pallas_call(
        functools.partial(_attn_body, t=t, seq=seq, unrolled=unrolled),
        out_shape=jax.ShapeDtypeStruct((nb * seq, ATT_VDIM), BF16),
        grid=grid, in_specs=in_specs,
        out_specs=pl.BlockSpec((tq, HEAD_W), out_map),
        scratch_shapes=[pltpu.VMEM((2, 2, 2, 2 * HEAD_W, t), BF16),
                        pltpu.VMEM((2, t, t), F32), pltpu.VMEM((2, t, t), F32),
                        pltpu.VMEM((2, HEAD_W + ONES_ROWS, t), F32),
                        pltpu.VMEM((t, t), F32)],
        compiler_params=_params(("arbitrary",) * len(grid)), name="attn",
    )(qT, k, vT, k_meta, vT_meta, slopes, lq1, lk1, lq2, lk2, attn_nw)


def _mlp_body(x_ref, att_ref, ssm_ref, wo_ref, n2_ref, wup_ref, wdown_ref, fw_ref, out_ref):
    h1 = (x_ref[...] + _dot(att_ref[...], wo_ref[0:ATT_VDIM, :])
          + _dot(ssm_ref[...], wo_ref[ATT_VDIM:ATT_VDIM + D_SSM, :]))
    u2 = _rms(h1, n2_ref[...]).astype(BF16)
    hid = jnp.maximum(_dot(u2, wup_ref[...]), 0.0)
    h2 = h1 + _dot((hid * hid).astype(BF16), wdown_ref[...])
    out_ref[...] = _rms(h2, fw_ref[...])


def _mlp(x2d, att, ssm, wo, n2, wup, wdown, fw, tm):
    rows = x2d.shape[0]
    tile = lambda w: pl.BlockSpec((tm, w), lambda i: (i, 0))
    return pl.pallas_call(
        _mlp_body, out_shape=jax.ShapeDtypeStruct((rows, D_MODEL), F32), grid=(rows // tm,),
        in_specs=[tile(D_MODEL), tile(ATT_VDIM), tile(D_SSM), _const_spec(wo.shape),
                  _const_spec((1, D_MODEL)), _const_spec(wup.shape), _const_spec(wdown.shape),
                  _const_spec((1, D_MODEL))],
        out_specs=tile(D_MODEL),
        compiler_params=_params(("arbitrary",)), name="mlp",
    )(x2d, att, ssm, wo, n2, wup, wdown, fw)


def _head_lane_vec(fwd_vals, bwd_vals):
    v = jnp.zeros((1, LANES), F32)
    v = v.at[0, 0:H_SSM].set(fwd_vals.astype(F32))
    return v.at[0, H_SSM:2 * H_SSM].set(bwd_vals.astype(F32))


def _expand_matrix(lane_lo):
    lane = jnp.arange(LANES)[:, None]
    ch = jnp.arange(D_SSM)[None, :] // SSM_HD
    return (lane == ch + lane_lo).astype(BF16)


def _trunk(x, meta_parts, p):
    nb, seq, _ = x.shape
    x2d = x.reshape(nb * seq, D_MODEL)
    k_meta, vT_meta, xbc_meta, dt_meta = meta_parts
    tm = min(ROW_TILE, seq)
    qT, k, vT, z, act, dt_raw, meta_act = _inproj(
        x2d, xbc_meta, p["n1"], p["w_in"], p["conv_w8"], p["conv_b"], nb, seq, tm)
    ssm = _ssd(act, meta_act, dt_raw, dt_meta, z, p["ssd_consts"], nb, seq)
    att = _attention(qT, k, vT, k_meta, vT_meta, p["slopes"], p["lq1"], p["lk1"], p["lq2"],
                     p["lk2"], p["attn_nw"], nb, seq)
    y = _mlp(x2d, att, ssm, p["wo"], p["n2"], p["wup"], p["wdown"], p["fw"], tm)
    return y.reshape(nb, seq, D_MODEL)


@jax.jit
def kernel(x_prompt, x_sample, meta_tokens, norm1_w, w_in, conv_w, conv_b, lambda_q1, lambda_k1,
           lambda_q2, lambda_k2, attn_norm_w, dt_bias_f, dt_bias_b, a_log_f, a_log_b, d_skip,
           ssm_norm_w, w_out, norm2_w, w_up, w_down, final_norm_w):
    li = 0
    w = w_in[li].astype(BF16)
    cuts = (0, ATT_VDIM, 2 * ATT_VDIM, 3 * ATT_VDIM, 3 * ATT_VDIM + D_SSM,
            3 * ATT_VDIM + D_SSM + CONV_DIM)
    wdt = jnp.zeros((D_MODEL, LANES), BF16).at[:, 0:2 * H_SSM].set(w[:, cuts[5]:])
    t = jnp.arange(CHUNK)
    p = {
        "n1": norm1_w[li].reshape(1, D_MODEL).astype(F32),
        "w_in": tuple(w[:, cuts[j]:cuts[j + 1]] for j in range(5)) + (wdt,),
        "conv_w8": jnp.zeros((HALO, CONV_DIM), F32).at[0:D_CONV].set(conv_w[li].astype(F32)),
        "conv_b": conv_b[li].reshape(1, CONV_DIM).astype(F32),
        "ssd_consts": (
            (t[:, None] >= t[None, :]).astype(BF16),
            (t[None, :] >= t[:, None]).astype(BF16),
            _expand_matrix(0), _expand_matrix(H_SSM),
            _head_lane_vec(dt_bias_f[li], dt_bias_b[li]),
            _head_lane_vec(a_log_f[li], a_log_b[li]),
            jnp.repeat(d_skip[li].astype(F32), SSM_HD).reshape(1, D_SSM),
            ssm_norm_w[li].reshape(1, D_SSM).astype(F32),
        ),
        "slopes": jnp.broadcast_to(
            (2.0 ** (-8.0 * (jnp.arange(N_HEADS, dtype=F32) + 1.0) / N_HEADS))[:, None, None],
            (N_HEADS, 1, LANES)),
        "lq1": lambda_q1[li].reshape(1, ATT_DH).astype(F32),
        "lk1": lambda_k1[li].reshape(1, ATT_DH).astype(F32),
        "lq2": lambda_q2[li].reshape(1, ATT_DH).astype(F32),
        "lk2": lambda_k2[li].reshape(1, ATT_DH).astype(F32),
        "attn_nw": attn_norm_w[li].reshape(1, HEAD_W).astype(F32),
        "wo": w_out[li].astype(BF16),
        "n2": norm2_w[li].reshape(1, D_MODEL).astype(F32),
        "wup": w_up[li].astype(BF16),
        "wdown": w_down[li].astype(BF16),
        "fw": final_norm_w.reshape(1, D_MODEL).astype(F32),
    }
    meta_block = jnp.concatenate(
        [jnp.zeros((NPAD, D_MODEL), F32), meta_tokens.astype(F32)], axis=0)
    _, k_m, vT_m, _, xbc_m, dt_m = _inproj_meta(meta_block, p["n1"], p["w_in"])
    meta_parts = (k_m, vT_m, xbc_m, dt_m)
    return (_trunk(x_prompt, meta_parts, p), _trunk(x_sample, meta_parts, p))
```

```python
import functools
import math

import jax
import jax.numpy as jnp
import numpy as np
from jax import lax
from jax.experimental import pallas as pl
from jax.experimental.pallas import tpu as pltpu

F32 = jnp.float32
BF16 = jnp.bfloat16

D_MODEL = 1024
N_META = 16
CHUNK = 128
NPAD = CHUNK - N_META
N_HEADS = 8
ATT_DH = 64
HEAD_W = 2 * ATT_DH
ATT_VDIM = N_HEADS * HEAD_W
D_SSM = 1024
H_SSM = 16
SSM_HD = 64
SSM_G = 2
SSM_N = 128
D_CONV = 5
CONV_DIM = D_SSM + 2 * SSM_G * SSM_N
D_FF = 4 * D_MODEL
EPS = 1e-5
LAM_INIT = 0.8 - 0.6 * math.exp(-0.3 * 0)
NEG = -1e30
LOG2E = math.log2(math.e)
BF16_EXACT_INTS = 256


def _bf16_parts(x, n):
    parts, r = [], np.float32(x)
    for _ in range(n):
        h = np.float32(np.asarray(r, dtype=jnp.bfloat16))
        parts.append(float(h))
        r = np.float32(r - h)
    return tuple(parts)


LOG2E_PARTS = _bf16_parts(LOG2E, 3)

LANES = 128
HALO = 8
VMEM_LIMIT = 56 * 1024 * 1024

ROW_TILE = 512
ONES_ROWS = 16
SSD_CHUNKS_PER_STEP = 4
ATTN_UNROLL_MAX_UNITS = 24
ATTN_UNROLL_SEQS = 2


def _dot(a, b):
    return jnp.dot(a, b, preferred_element_type=F32)


def _rms(x, w):
    return x * lax.rsqrt(jnp.mean(x * x, axis=-1, keepdims=True) + EPS) * w


def _silu(x):
    return x / (1.0 + jnp.exp(-x))


def _const_spec(shape):
    nd = len(shape)
    return pl.BlockSpec(shape, lambda *_: (0,) * nd)


def _params(sem, flags=None):
    return pltpu.CompilerParams(dimension_semantics=sem, vmem_limit_bytes=VMEM_LIMIT, flags=flags)


def _conv_taps(ext, w, b, rows):
    n = ext.shape[0]
    acc = b
    for j in range(D_CONV):
        shifted = ext if j == 2 else pltpu.roll(ext, (2 - j) % n, 0)
        acc = acc + w[j:j + 1, :] * shifted[HALO:HALO + rows, :]
    return _silu(acc)


def _project(u, wq_ref, wk_ref, wv_ref, wz_ref, wdt_ref, qT_ref, k_ref, vT_ref, z_ref, dt_ref):
    q = _dot(u, wq_ref[...]) * (ATT_DH ** -0.5 * LOG2E)
    for h in range(N_HEADS):
        qT_ref[h] = q[:, h * HEAD_W:(h + 1) * HEAD_W].T.astype(BF16)
    k = _dot(u, wk_ref[...])
    for h in range(N_HEADS):
        k_ref[h] = k[:, h * HEAD_W:(h + 1) * HEAD_W].astype(BF16)
    v = _dot(u, wv_ref[...])
    for h in range(N_HEADS):
        vT_ref[h, 0] = v[:, h * HEAD_W:(h + 1) * HEAD_W].T.astype(BF16)
    z_ref[...] = _dot(u, wz_ref[...])
    dt_ref[...] = _dot(u, wdt_ref[...])


def _inproj_meta_body(x_ref, n1_ref, wq_ref, wk_ref, wv_ref, wz_ref, wx_ref, wdt_ref,
                      qT_ref, k_ref, vT_ref, z_ref, xbc_ref, dt_ref):
    u = _rms(x_ref[...], n1_ref[...]).astype(BF16)
    _project(u, wq_ref, wk_ref, wv_ref, wz_ref, wdt_ref, qT_ref, k_ref, vT_ref, z_ref, dt_ref)
    xbc_ref[...] = _dot(u, wx_ref[...])


def _inproj_body(x_ref, xprev_ref, xnext_ref, metaraw_ref, n1_ref, wq_ref, wk_ref, wv_ref, wz_ref,
                 wx_ref, wdt_ref, cw_ref, cb_ref,
                 qT_ref, k_ref, vT_ref, z_ref, act_ref, dt_ref, metaact_ref, *, tm, nt):
    i = pl.program_id(1)
    n1 = n1_ref[...]
    x_ext = jnp.concatenate([xprev_ref[...], x_ref[...], xnext_ref[...]], axis=0)
    u_ext = _rms(x_ext, n1).astype(BF16)
    raw = _dot(u_ext, wx_ref[...])
    w = cw_ref[...]
    b = cb_ref[...]
    prev = jnp.where(i == 0, metaraw_ref[CHUNK - HALO:CHUNK, :], raw[0:HALO, :])
    nxt = jnp.where(i == nt - 1, 0.0, raw[HALO + tm:, :])
    main = raw[HALO:HALO + tm, :]
    act_ref[...] = _conv_taps(jnp.concatenate([prev, main, nxt], axis=0), w, b, tm)
    _project(u_ext[HALO:HALO + tm, :], wq_ref, wk_ref, wv_ref, wz_ref, wdt_ref,
             qT_ref, k_ref, vT_ref, z_ref, dt_ref)

    @pl.when(i == 0)
    def _():
        zeros = jnp.zeros((HALO, CONV_DIM), F32)
        extm = jnp.concatenate([zeros, metaraw_ref[...], main[0:HALO, :]], axis=0)
        am = _conv_taps(extm, w, b, CHUNK)
        row = lax.broadcasted_iota(jnp.int32, (CHUNK, 1), 0)
        metaact_ref[0] = jnp.where(row >= NPAD, am, 0.0)


def _inproj_specs(rows, tm, tile_map):
    out_shape = (
        jax.ShapeDtypeStruct((N_HEADS, HEAD_W, rows), BF16),
        jax.ShapeDtypeStruct((N_HEADS, rows, HEAD_W), BF16),
        jax.ShapeDtypeStruct((N_HEADS, rows // tm, HEAD_W, tm), BF16),
        jax.ShapeDtypeStruct((rows, D_SSM), F32),
        jax.ShapeDtypeStruct((rows, CONV_DIM), F32),
        jax.ShapeDtypeStruct((rows, LANES), F32),
    )
    at = lambda f: (lambda *g: f(tile_map(*g)))
    out_specs = (
        pl.BlockSpec((N_HEADS, HEAD_W, tm), at(lambda r: (0, 0, r))),
        pl.BlockSpec((N_HEADS, tm, HEAD_W), at(lambda r: (0, r, 0))),
        pl.BlockSpec((N_HEADS, 1, HEAD_W, tm), at(lambda r: (0, r, 0, 0))),
        pl.BlockSpec((tm, D_SSM), at(lambda r: (r, 0))),
        pl.BlockSpec((tm, CONV_DIM), at(lambda r: (r, 0))),
        pl.BlockSpec((tm, LANES), at(lambda r: (r, 0))),
    )
    return out_shape, out_specs


def _inproj_meta(meta_block, n1, w_in):
    out_shape, out_specs = _inproj_specs(CHUNK, CHUNK, lambda i: i)
    return pl.pallas_call(
        _inproj_meta_body, out_shape=out_shape, grid=(1,),
        in_specs=[pl.BlockSpec((CHUNK, D_MODEL), lambda i: (i, 0)), _const_spec((1, D_MODEL))]
        + [_const_spec(w.shape) for w in w_in],
        out_specs=out_specs, compiler_params=_params(("arbitrary",)), name="inproj_meta",
    )(meta_block, n1, *w_in)


def _inproj(x2d, meta_raw, n1, w_in, conv_w8, conv_b, nb, seq, tm):
    nt = seq // tm
    hb = tm // HALO
    nblk8 = (nb * seq) // HALO
    out_shape, out_specs = _inproj_specs(nb * seq, tm, lambda b, i: b * nt + i)
    out_shape += (jax.ShapeDtypeStruct((nb, CHUNK, CONV_DIM), F32),)
    out_specs += (pl.BlockSpec((1, CHUNK, CONV_DIM), lambda b, i: (b, 0, 0)),)
    in_specs = [
        pl.BlockSpec((tm, D_MODEL), lambda b, i: (b * nt + i, 0)),
        pl.BlockSpec((HALO, D_MODEL), lambda b, i: (jnp.maximum((b * nt + i) * hb - 1, 0), 0)),
        pl.BlockSpec((HALO, D_MODEL),
                     lambda b, i: (jnp.minimum((b * nt + i + 1) * hb, nblk8 - 1), 0)),
        _const_spec((CHUNK, CONV_DIM)), _const_spec((1, D_MODEL)),
    ] + [_const_spec(w.shape) for w in w_in] + [_const_spec((HALO, CONV_DIM)),
                                                 _const_spec((1, CONV_DIM))]
    return pl.pallas_call(
        functools.partial(_inproj_body, tm=tm, nt=nt), out_shape=out_shape, grid=(nb, nt),
        in_specs=in_specs, out_specs=out_specs,
        compiler_params=_params(("arbitrary", "arbitrary")), name="inproj",
    )(x2d, x2d, x2d, meta_raw, n1, *w_in, conv_w8, conv_b)


def _split3(x):
    hi = x.astype(BF16)
    r = x - hi.astype(F32)
    mid = r.astype(BF16)
    lo = (r - mid.astype(F32)).astype(BF16)
    return hi, mid, lo


def _expand_heads(x, e2):
    hi = x.astype(BF16)
    lo = (x - hi.astype(F32)).astype(BF16)
    return _dot(jnp.concatenate([hi, lo], axis=1), e2)


def _ssd_chunk(act, dtr, valid, tri, e, biasv, alogv, lane_lo, st_ref, fwd, want_y=True):
    xs = act[:, :D_SSM]
    lane = lax.broadcasted_iota(jnp.int32, (1, LANES), 1)
    hmask = (lane >= lane_lo) & (lane < lane_lo + H_SSM)
    x = dtr + biasv
    dt = jnp.maximum(x, 0.0) + jnp.log(1.0 + jnp.exp(-jnp.abs(x)))
    dt = jnp.where(hmask & valid, dt, 0.0)
    adt = dt * (-jnp.exp(alogv))
    hi, mid, lo = _split3(adt)
    acs = (_dot(jnp.concatenate([tri, tri], axis=1), jnp.concatenate([hi, mid], axis=0))
           + _dot(tri, lo))
    acs_t = acs.T
    edge = acs[CHUNK - 1:CHUNK, :] if fwd else acs[0:1, :]
    e2 = jnp.concatenate([e, e], axis=0)
    dt_x = _expand_heads(dt, e2)
    ein_x = _expand_heads(jnp.exp(acs), e2)
    est_x = _expand_heads(jnp.exp(edge - acs), e2)
    cd_x = ein_x[CHUNK - 1:CHUNK, :] if fwd else ein_x[0:1, :]
    xg = xs * dt_x
    xgd = (xg * est_x).astype(BF16)
    li = lax.broadcasted_iota(jnp.int32, (CHUNK, CHUNK), 0)
    si = lax.broadcasted_iota(jnp.int32, (CHUNK, CHUNK), 1)
    tmask = (li >= si) if fwd else (si >= li)
    left = lax.broadcasted_iota(jnp.int32, (1, LANES), 1) < SSM_HD
    ys = []
    heads_per_group = H_SSM // SSM_G
    for g in range(SSM_G):
        bg = act[:, D_SSM + g * SSM_N:D_SSM + (g + 1) * SSM_N]
        cg = act[:, D_SSM + SSM_G * SSM_N + g * SSM_N:D_SSM + SSM_G * SSM_N + (g + 1) * SSM_N]
        cb = cg.astype(BF16)
        bb = bg.astype(BF16)
        st = st_ref[g]
        gsl = slice(g * heads_per_group * SSM_HD, (g + 1) * heads_per_group * SSM_HD)
        st_ref[g] = st * cd_x[:, gsl] + _dot(bg.T.astype(BF16), xgd[:, gsl])
        if not want_y:
            continue
        gm = lax.dot_general(cb, bb, (((1,), (1,)), ((), ())), preferred_element_type=F32)
        yoff = _dot(cb, st.astype(BF16))
        for jp in range(heads_per_group // 2):
            pair = g * (heads_per_group // 2) + jp
            sl = slice(pair * LANES, (pair + 1) * LANES)
            xg_pair = xg[:, sl]
            mds = []
            for hh in range(2):
                c = lane_lo + pair * 2 + hh
                seg = acs[:, c:c + 1] - acs_t[c:c + 1, :]
                decay = jnp.exp(jnp.where(tmask, seg, NEG))
                mds.append((gm * decay).astype(BF16))
            xg2 = jnp.concatenate([jnp.where(left, xg_pair, 0.0).astype(BF16),
                                   jnp.where(left, 0.0, xg_pair).astype(BF16)], axis=0)
            ys.append(ein_x[:, sl] * yoff[:, jp * LANES:(jp + 1) * LANES]
                      + _dot(jnp.concatenate(mds, axis=1), xg2))
    return (jnp.concatenate(ys, axis=1) if want_y else None), xs


def _ssd_fwd_body(actx_ref, actm_ref, dtx_ref, dtm_ref, tri_ref, e_ref, biasv_ref, alogv_ref,
                  yf_ref, st_ref):
    consts = (tri_ref[...], e_ref[...], biasv_ref[...], alogv_ref[...], 0, st_ref, True)

    @pl.when(pl.program_id(1) == 0)
    def _():
        st_ref[...] = jnp.zeros_like(st_ref)
        row = lax.broadcasted_iota(jnp.int32, (CHUNK, 1), 0)
        _ssd_chunk(actm_ref[0], dtm_ref[...], row >= NPAD, *consts, want_y=False)

    for sub in range(SSD_CHUNKS_PER_STEP):
        rows = slice(sub * CHUNK, (sub + 1) * CHUNK)
        y, _ = _ssd_chunk(actx_ref[rows, :], dtx_ref[rows, :], True, *consts)
        yf_ref[rows, :] = y


def _ssd_bwd_body(act_ref, dt_ref, yf_ref, z_ref, tri_ref, e_ref, biasv_ref, alogv_ref,
                  dskip_ref, nw_ref, out_ref, st_ref):
    @pl.when(pl.program_id(1) == 0)
    def _():
        st_ref[...] = jnp.zeros_like(st_ref)

    for sub in reversed(range(SSD_CHUNKS_PER_STEP)):
        rows = slice(sub * CHUNK, (sub + 1) * CHUNK)
        yb, xs = _ssd_chunk(act_ref[rows, :], dt_ref[rows, :], True, tri_ref[...], e_ref[...],
                            biasv_ref[...], alogv_ref[...], H_SSM, st_ref, False)
        y = yf_ref[rows, :] + yb + xs * dskip_ref[...]
        out_ref[rows, :] = _rms(y * _silu(z_ref[rows, :]), nw_ref[...]).astype(BF16)


def _ssd(act, meta_act, dt_raw, dt_meta, z, consts, nb, seq):
    step = SSD_CHUNKS_PER_STEP * CHUNK
    ns = seq // step
    tril, triu, e_f, e_b, biasv, alogv, dskip_x, ssm_nw = consts
    st_shape = pltpu.VMEM((SSM_G, SSM_N, (H_SSM // SSM_G) * SSM_HD), F32)

    xblk = lambda b, c: (b * ns + c, 0)
    y_f = pl.pallas_call(
        _ssd_fwd_body,
        out_shape=jax.ShapeDtypeStruct((nb * seq, D_SSM), F32),
        grid=(nb, ns),
        in_specs=[
            pl.BlockSpec((step, CONV_DIM), xblk),
            pl.BlockSpec((1, CHUNK, CONV_DIM), lambda b, c: (b, 0, 0)),
            pl.BlockSpec((step, LANES), xblk),
            _const_spec((CHUNK, LANES)),
            _const_spec((CHUNK, CHUNK)), _const_spec((LANES, D_SSM)),
            _const_spec((1, LANES)), _const_spec((1, LANES)),
        ],
        out_specs=pl.BlockSpec((step, D_SSM), xblk),
        scratch_shapes=[st_shape],
        compiler_params=_params(("arbitrary", "arbitrary")), name="ssd_fwd",
    )(act, meta_act, dt_raw, dt_meta, tril, e_f, biasv, alogv)

    rblk = lambda b, c: (b * ns + ns - 1 - c, 0)
    return pl.pallas_call(
        _ssd_bwd_body,
        out_shape=jax.ShapeDtypeStruct((nb * seq, D_SSM), BF16),
        grid=(nb, ns),
        in_specs=[
            pl.BlockSpec((step, CONV_DIM), rblk),
            pl.BlockSpec((step, LANES), rblk),
            pl.BlockSpec((step, D_SSM), rblk),
            pl.BlockSpec((step, D_SSM), rblk),
            _const_spec((CHUNK, CHUNK)), _const_spec((LANES, D_SSM)),
            _const_spec((1, LANES)), _const_spec((1, LANES)),
            _const_spec((1, D_SSM)), _const_spec((1, D_SSM)),
        ],
        out_specs=pl.BlockSpec((step, D_SSM), rblk),
        scratch_shapes=[st_shape],
        compiler_params=_params(("arbitrary", "arbitrary")), name="ssd_bwd",
    )(act, dt_raw, y_f, z, triu, e_b, biasv, alogv, dskip_x, ssm_nw)


def _attn_body(qT_ref, k_ref, vT_ref, km_ref, vmT_ref, slope_ref, lq1_ref, lk1_ref, lq2_ref,
               lk2_ref, nw_ref, out_ref, qaug_ref, sa_ref, sb_ref, acc_ref, dbias_ref, feat_ref,
               kfeat_ref,
               *, t, seq, unrolled):
    nkv = seq // t
    slope = slope_ref[0]
    slope_q = slope[:, 0:1]
    row = lax.broadcasted_iota(jnp.int32, (HEAD_W, 1), 0)

    def log2e_part(idx):
        part = idx % 3
        return jnp.where(part == 0, LOG2E_PARTS[0],
                         jnp.where(part == 1, LOG2E_PARTS[1], LOG2E_PARTS[2]))

    def key_feat(rows, rel, valid):
        lane = lax.broadcasted_iota(jnp.int32, (rows, LANES), 1)
        r_hi = (rel // BF16_EXACT_INTS) * BF16_EXACT_INTS
        f = jnp.where(lane < 3, slope * r_hi.astype(F32),
                      jnp.where(lane < 6, slope * (rel - r_hi).astype(F32),
                                jnp.where(lane < 12, log2e_part(lane), 0.0)))
        if valid is not None:
            f = jnp.where(valid, f, jnp.where(lane == 3, NEG, 0.0))
        return f.astype(BF16)

    @pl.when(jnp.logical_and(pl.program_id(1) == 0, pl.program_id(2) == 0))
    def _():
        frow = lax.broadcasted_iota(jnp.int32, (HEAD_W, t), 0)
        fcol = lax.broadcasted_iota(jnp.int32, (HEAD_W, t), 1)
        c_hi = (fcol // BF16_EXACT_INTS) * BF16_EXACT_INTS
        feat = jnp.where(frow < 6, log2e_part(frow),
                         jnp.where(frow < 9, -slope_q * c_hi.astype(F32),
                                   jnp.where(frow < 12, -slope_q * (fcol - c_hi).astype(F32), 0.0)))
        feat_ref[0] = feat.astype(BF16)
        feat_ref[1] = (-feat).astype(BF16)
        kfeat_ref[0:t, :] = key_feat(t, lax.broadcasted_iota(jnp.int32, (t, LANES), 0), None)
        mrow = lax.broadcasted_iota(jnp.int32, (CHUNK, LANES), 0)
        kfeat_ref[t:t + CHUNK, :] = key_feat(CHUNK, mrow - CHUNK, mrow >= NPAD)
        d = (lax.broadcasted_iota(jnp.int32, (t, t), 0)
             - lax.broadcasted_iota(jnp.int32, (t, t), 1))
        dbias_ref[...] = jnp.where(d > 0, (-2.0 * LOG2E * slope_q) * d.astype(F32), 0.0)

    def set_queries(qT, slot):
        zero = jnp.zeros_like(qT)
        qz = (jnp.where(row < ATT_DH, qT, zero), jnp.where(row >= ATT_DH, qT, zero))
        for sg in range(2):
            for mp in range(2):
                qaug_ref[slot, sg, mp] = jnp.concatenate([qz[mp], feat_ref[sg]], axis=0)

    ones_rows = jnp.ones((ONES_ROWS, t), BF16)
    ones_m = jnp.ones((ONES_ROWS, CHUNK), BF16)
    lam = (jnp.exp(jnp.sum(lq1_ref[...] * lk1_ref[...], axis=-1, keepdims=True))
           - jnp.exp(jnp.sum(lq2_ref[...] * lk2_ref[...], axis=-1, keepdims=True)) + LAM_INIT)
    slots = (sa_ref, sb_ref)

    def as_f32(x):
        return float(x) if isinstance(x, int) else x.astype(F32)

    def x_lhs(j):
        k0 = j * t if isinstance(j, int) else pl.multiple_of(j * t, t)
        return jnp.concatenate([k_ref[0, pl.ds(k0, t), :], kfeat_ref[0:t, :]], axis=1)

    def unit_shift(unit, i):
        if unit[0] == "x":
            d = unit[1] - i
            return (LOG2E * slope_q) * as_f32(-(abs(d) if isinstance(d, int) else jnp.abs(d)) * t)
        return (LOG2E * slope_q) * as_f32(-(i * t)) if unit[0] == "meta" else 0.0

    def unit_rows(unit):
        return CHUNK if unit[0] == "meta" else t

    def produce(unit, i, base, qslot, s_ref):
        if unit[0] == "x":
            j = unit[1]
            after = j > i
            sg = int(after) if isinstance(after, bool) else after.astype(jnp.int32)
            lhs = x_lhs(base + j)
            blocks = [_dot(lhs, qaug_ref[qslot, sg, mp]) for mp in range(2)]
        elif unit[0] == "meta":
            lhs = jnp.concatenate([km_ref[0], kfeat_ref[t:t + CHUNK, :]], axis=1)
            blocks = [_dot(lhs, qaug_ref[qslot, 0, mp]) for mp in range(2)]
        else:
            lhs = x_lhs(base + i)
            blocks = [_dot(lhs, qaug_ref[qslot, 0, mp]) + dbias_ref[...] for mp in range(2)]
        rows = unit_rows(unit)
        mbs = []
        for mp, s in enumerate(blocks):
            s_ref[mp, 0:rows, :] = s
            mbs.append(jnp.max(s, axis=0, keepdims=True))
        return tuple(mbs)

    def consume(unit, i, base, s_ref, mbs, ms):
        cterm = unit_shift(unit, i)
        rows = unit_rows(unit)
        if unit[0] == "meta":
            v_aug = jnp.concatenate([vmT_ref[0, 0], ones_m], axis=0)
        else:
            j = unit[1] if unit[0] == "x" else i
            v_aug = jnp.concatenate([vT_ref[0, base + j], ones_rows], axis=0)
        out = []
        for mp in range(2):
            m_new = jnp.maximum(ms[mp], mbs[mp] + cterm)
            p = jnp.exp2(s_ref[mp, 0:rows, :] - (m_new - cterm)).astype(BF16)
            alpha = jnp.exp2(ms[mp] - m_new)
            acc_ref[mp] = acc_ref[mp] * alpha + _dot(v_aug, p)
            out.append(m_new)
        return tuple(out)

    def start_tile():
        acc_ref[...] = jnp.zeros_like(acc_ref)
        return (jnp.full((1, t), NEG, F32),) * 2

    def finalize(rows):
        o0 = acc_ref[0, 0:HEAD_W, :] / acc_ref[0, HEAD_W:HEAD_W + 1, :]
        o1 = acc_ref[1, 0:HEAD_W, :] / acc_ref[1, HEAD_W:HEAD_W + 1, :]
        o = o0 - lam * o1
        on = o * lax.rsqrt(jnp.mean(o * o, axis=0, keepdims=True) + EPS)
        out_ref[rows, :] = ((on.T * nw_ref[...]) * (1.0 - LAM_INIT)).astype(BF16)

    def run_static(units, pending, ms):
        for u, (l, i, base, unit) in enumerate(units):
            nxt = None
            if u + 1 < len(units):
                nl, ni, nbase, nunit = units[u + 1]
                if nl != l:
                    set_queries(qT_ref[0, :, nl * t:(nl + 1) * t], nl % 2)
                nxt = produce(nunit, ni, nbase, nl % 2, slots[(u + 1) % 2])
            if unrolled and (u == 0 or units[u - 1][0] != l):
                ms = start_tile()
            ms = consume(unit, i, base, slots[u % 2], pending, ms)
            pending = nxt
            if unrolled and (u + 1 == len(units) or units[u + 1][0] != l):
                finalize(slice(l * t, (l + 1) * t))
        return ms

    def x_units(i):
        return [("x", n + (n >= i)) for n in range(nkv - 1)]

    tail = [("diag",), ("meta",)]
    if unrolled:
        units = []
        for l in range(unrolled):
            i, base = l % nkv, (l // nkv) * nkv
            units += [(l, i, base, u) for u in x_units(i) + tail]
        set_queries(qT_ref[0, :, 0:t], 0)
        run_static(units, produce(units[0][3], 0, 0, 0, sa_ref), None)
    else:
        i = pl.program_id(2)
        set_queries(qT_ref[0], 0)
        ms = start_tile()
        n_off = nkv - 1
        xunit = lambda n: ("x", n + (n >= i).astype(jnp.int32))
        rest = tail
        if n_off >= 1:
            mbs = produce(xunit(0), i, 0, 0, sa_ref)
            pairs = (n_off - 1) // 2

            def pair_body(u, carry):
                ms, mbs = carry[:2], carry[2:]
                n = 2 * u
                nxt = produce(xunit(n + 1), i, 0, 0, sb_ref)
                ms = consume(xunit(n), i, 0, sa_ref, mbs, ms)
                mbs = produce(xunit(n + 2), i, 0, 0, sa_ref)
                ms = consume(xunit(n + 1), i, 0, sb_ref, nxt, ms)
                return ms + mbs

            unroll = next((f for f in (5, 3) if pairs % f == 0), 1)
            carry = lax.fori_loop(0, pairs, pair_body, ms + mbs, unroll=unroll)
            ms, mbs = carry[:2], carry[2:]
            rest = [xunit(n) for n in range(2 * pairs, n_off)] + tail
        else:
            mbs = produce(rest[0], i, 0, 0, sa_ref)
        run_static([(0, i, 0, u) for u in rest], mbs, ms)
        finalize(slice(None))


def _attention(qT, k, vT, k_meta, vT_meta, slopes, lq1, lk1, lq2, lk2, attn_nw, nb, seq):
    t = min(ROW_TILE, seq)
    nq = seq // t
    unrolled, group = 0, 1
    if nq * (nq + 1) <= ATTN_UNROLL_MAX_UNITS:
        group = ATTN_UNROLL_SEQS if nb % ATTN_UNROLL_SEQS == 0 else 1
        unrolled = group * nq
    tq = max(unrolled, 1) * t
    steps = (group * seq) // tq
    grid = (N_HEADS, nb // group, steps)
    q_map = lambda h, b, i: (h, 0, b * steps + i)
    out_map = lambda h, b, i: (b * steps + i, h)
    per_seq = lambda f: (lambda h, b, i: f(h, b))
    vec = lambda n: _const_spec((1, n))
    in_specs = [
        pl.BlockSpec((1, HEAD_W, tq), q_map),
        pl.BlockSpec((1, group * seq, HEAD_W), per_seq(lambda h, b: (h, b, 0))),
        pl.BlockSpec((1, group * nq, HEAD_W, t), per_seq(lambda h, b: (h, b, 0, 0))),
        pl.BlockSpec((1, CHUNK, HEAD_W), per_seq(lambda h, b: (h, 0, 0))),
        pl.BlockSpec((1, 1, HEAD_W, CHUNK), per_seq(lambda h, b: (h, 0, 0, 0))),
        pl.BlockSpec((1, 1, LANES), per_seq(lambda h, b: (h, 0, 0))),
        vec(ATT_DH), vec(ATT_DH), vec(ATT_DH), vec(ATT_DH), vec(HEAD_W),
    ]
    return pl.pallas_call(
        functools.partial(_attn_body, t=t, seq=seq, unrolled=unrolled),
        out_shape=jax.ShapeDtypeStruct((nb * seq, ATT_VDIM), BF16),
        grid=grid, in_specs=in_specs,
        out_specs=pl.BlockSpec((tq, HEAD_W), out_map),
        scratch_shapes=[pltpu.VMEM((2, 2, 2, 2 * HEAD_W, t), BF16),
                        pltpu.VMEM((2, t, t), F32), pltpu.VMEM((2, t, t), F32),
                        pltpu.VMEM((2, HEAD_W + ONES_ROWS, t), F32),
                        pltpu.VMEM((t, t), F32),
                        pltpu.VMEM((2, HEAD_W, t), BF16),
                        pltpu.VMEM((t + CHUNK, LANES), BF16)],
        compiler_params=_params(("arbitrary",) * len(grid)), name="attn",
    )(qT, k, vT, k_meta, vT_meta, slopes, lq1, lk1, lq2, lk2, attn_nw)


def _mlp_body(x_ref, att_ref, ssm_ref, wo_ref, n2_ref, wup_ref, wdown_ref, fw_ref, out_ref):
    h1 = (x_ref[...] + _dot(att_ref[...], wo_ref[0:ATT_VDIM, :])
          + _dot(ssm_ref[...], wo_ref[ATT_VDIM:ATT_VDIM + D_SSM, :]))
    u2 = _rms(h1, n2_ref[...]).astype(BF16)
    hid = jnp.maximum(_dot(u2, wup_ref[...]), 0.0)
    h2 = h1 + _dot((hid * hid).astype(BF16), wdown_ref[...])
    out_ref[...] = _rms(h2, fw_ref[...])


def _mlp(x2d, att, ssm, wo, n2, wup, wdown, fw, tm):
    rows = x2d.shape[0]
    tile = lambda w: pl.BlockSpec((tm, w), lambda i: (i, 0))
    return pl.pallas_call(
        _mlp_body, out_shape=jax.ShapeDtypeStruct((rows, D_MODEL), F32), grid=(rows // tm,),
        in_specs=[tile(D_MODEL), tile(ATT_VDIM), tile(D_SSM), _const_spec(wo.shape),
                  _const_spec((1, D_MODEL)), _const_spec(wup.shape), _const_spec(wdown.shape),
                  _const_spec((1, D_MODEL))],
        out_specs=tile(D_MODEL),
        compiler_params=_params(("arbitrary",)), name="mlp",
    )(x2d, att, ssm, wo, n2, wup, wdown, fw)


def _head_lane_vec(fwd_vals, bwd_vals):
    v = jnp.zeros((1, LANES), F32)
    v = v.at[0, 0:H_SSM].set(fwd_vals.astype(F32))
    return v.at[0, H_SSM:2 * H_SSM].set(bwd_vals.astype(F32))


def _expand_matrix(lane_lo):
    lane = jnp.arange(LANES)[:, None]
    ch = jnp.arange(D_SSM)[None, :] // SSM_HD
    return (lane == ch + lane_lo).astype(BF16)


def _trunk(x, meta_parts, p):
    nb, seq, _ = x.shape
    x2d = x.reshape(nb * seq, D_MODEL)
    k_meta, vT_meta, xbc_meta, dt_meta = meta_parts
    tm = min(ROW_TILE, seq)
    qT, k, vT, z, act, dt_raw, meta_act = _inproj(
        x2d, xbc_meta, p["n1"], p["w_in"], p["conv_w8"], p["conv_b"], nb, seq, tm)
    ssm = _ssd(act, meta_act, dt_raw, dt_meta, z, p["ssd_consts"], nb, seq)
    att = _attention(qT, k, vT, k_meta, vT_meta, p["slopes"], p["lq1"], p["lk1"], p["lq2"],
                     p["lk2"], p["attn_nw"], nb, seq)
    y = _mlp(x2d, att, ssm, p["wo"], p["n2"], p["wup"], p["wdown"], p["fw"], tm)
    return y.reshape(nb, seq, D_MODEL)


@jax.jit
def kernel(x_prompt, x_sample, meta_tokens, norm1_w, w_in, conv_w, conv_b, lambda_q1, lambda_k1,
           lambda_q2, lambda_k2, attn_norm_w, dt_bias_f, dt_bias_b, a_log_f, a_log_b, d_skip,
           ssm_norm_w, w_out, norm2_w, w_up, w_down, final_norm_w):
    li = 0
    w = w_in[li].astype(BF16)
    cuts = (0, ATT_VDIM, 2 * ATT_VDIM, 3 * ATT_VDIM, 3 * ATT_VDIM + D_SSM,
            3 * ATT_VDIM + D_SSM + CONV_DIM)
    wdt = jnp.zeros((D_MODEL, LANES), BF16).at[:, 0:2 * H_SSM].set(w[:, cuts[5]:])
    t = jnp.arange(CHUNK)
    p = {
        "n1": norm1_w[li].reshape(1, D_MODEL).astype(F32),
        "w_in": tuple(w[:, cuts[j]:cuts[j + 1]] for j in range(5)) + (wdt,),
        "conv_w8": jnp.zeros((HALO, CONV_DIM), F32).at[0:D_CONV].set(conv_w[li].astype(F32)),
        "conv_b": conv_b[li].reshape(1, CONV_DIM).astype(F32),
        "ssd_consts": (
            (t[:, None] >= t[None, :]).astype(BF16),
            (t[None, :] >= t[:, None]).astype(BF16),
            _expand_matrix(0), _expand_matrix(H_SSM),
            _head_lane_vec(dt_bias_f[li], dt_bias_b[li]),
            _head_lane_vec(a_log_f[li], a_log_b[li]),
            jnp.repeat(d_skip[li].astype(F32), SSM_HD).reshape(1, D_SSM),
            ssm_norm_w[li].reshape(1, D_SSM).astype(F32),
        ),
        "slopes": jnp.broadcast_to(
            (2.0 ** (-8.0 * (jnp.arange(N_HEADS, dtype=F32) + 1.0) / N_HEADS))[:, None, None],
            (N_HEADS, 1, LANES)),
        "lq1": lambda_q1[li].reshape(1, ATT_DH).astype(F32),
        "lk1": lambda_k1[li].reshape(1, ATT_DH).astype(F32),
        "lq2": lambda_q2[li].reshape(1, ATT_DH).astype(F32),
        "lk2": lambda_k2[li].reshape(1, ATT_DH).astype(F32),
        "attn_nw": attn_norm_w[li].reshape(1, HEAD_W).astype(F32),
        "wo": w_out[li].astype(BF16),
        "n2": norm2_w[li].reshape(1, D_MODEL).astype(F32),
        "wup": w_up[li].astype(BF16),
        "wdown": w_down[li].astype(BF16),
        "fw": final_norm_w.reshape(1, D_MODEL).astype(F32),
    }
    meta_block = jnp.concatenate(
        [jnp.zeros((NPAD, D_MODEL), F32), meta_tokens.astype(F32)], axis=0)
    _, k_m, vT_m, _, xbc_m, dt_m = _inproj_meta(meta_block, p["n1"], p["w_in"])
    meta_parts = (k_m, vT_m, xbc_m, dt_m)
    return (_trunk(x_prompt, meta_parts, p), _trunk(x_sample, meta_parts, p))
```

```python
import functools
import math

import jax
import jax.numpy as jnp
import numpy as np
from jax import lax
from jax.experimental import pallas as pl
from jax.experimental.pallas import tpu as pltpu

F32 = jnp.float32
BF16 = jnp.bfloat16

D_MODEL = 1024
N_META = 16
CHUNK = 128
NPAD = CHUNK - N_META
N_HEADS = 8
ATT_DH = 64
HEAD_W = 2 * ATT_DH
ATT_VDIM = N_HEADS * HEAD_W
D_SSM = 1024
H_SSM = 16
SSM_HD = 64
SSM_G = 2
SSM_N = 128
D_CONV = 5
CONV_DIM = D_SSM + 2 * SSM_G * SSM_N
D_FF = 4 * D_MODEL
EPS = 1e-5
LAM_INIT = 0.8 - 0.6 * math.exp(-0.3 * 0)
NEG = -1e30
LOG2E = math.log2(math.e)
BF16_EXACT_INTS = 256


def _bf16_parts(x, n):
    parts, r = [], np.float32(x)
    for _ in range(n):
        h = np.float32(np.asarray(r, dtype=jnp.bfloat16))
        parts.append(float(h))
        r = np.float32(r - h)
    return tuple(parts)


LOG2E_PARTS = _bf16_parts(LOG2E, 3)

LANES = 128
HALO = 8
VMEM_LIMIT = 56 * 1024 * 1024

ROW_TILE = 512
ONES_ROWS = 16
SSD_CHUNKS_PER_STEP = 4
ATTN_UNROLL_MAX_UNITS = 24
ATTN_UNROLL_SEQS = 2


def _dot(a, b):
    return jnp.dot(a, b, preferred_element_type=F32)


def _rms(x, w):
    return x * lax.rsqrt(jnp.mean(x * x, axis=-1, keepdims=True) + EPS) * w


def _silu(x):
    return x / (1.0 + jnp.exp(-x))


def _const_spec(shape):
    nd = len(shape)
    return pl.BlockSpec(shape, lambda *_: (0,) * nd)


def _params(sem, flags=None):
    return pltpu.CompilerParams(dimension_semantics=sem, vmem_limit_bytes=VMEM_LIMIT, flags=flags)


def _conv_taps(ext, w, b, rows):
    n = ext.shape[0]
    acc = b
    for j in range(D_CONV):
        shifted = ext if j == 2 else pltpu.roll(ext, (2 - j) % n, 0)
        acc = acc + w[j:j + 1, :] * shifted[HALO:HALO + rows, :]
    return _silu(acc)


def _project(u, wq_ref, wk_ref, wv_ref, wz_ref, wdt_ref, qT_ref, k_ref, vT_ref, z_ref, dt_ref):
    q = _dot(u, wq_ref[...]) * (ATT_DH ** -0.5 * LOG2E)
    for h in range(N_HEADS):
        qT_ref[h] = q[:, h * HEAD_W:(h + 1) * HEAD_W].T.astype(BF16)
    k = _dot(u, wk_ref[...])
    for h in range(N_HEADS):
        k_ref[h] = k[:, h * HEAD_W:(h + 1) * HEAD_W].astype(BF16)
    v = _dot(u, wv_ref[...])
    for h in range(N_HEADS):
        vT_ref[h, 0] = v[:, h * HEAD_W:(h + 1) * HEAD_W].T.astype(BF16)
    z_ref[...] = _dot(u, wz_ref[...])
    dt_ref[...] = _dot(u, wdt_ref[...])


def _inproj_meta_body(x_ref, n1_ref, wq_ref, wk_ref, wv_ref, wz_ref, wx_ref, wdt_ref,
                      qT_ref, k_ref, vT_ref, z_ref, xbc_ref, dt_ref):
    u = _rms(x_ref[...], n1_ref[...]).astype(BF16)
    _project(u, wq_ref, wk_ref, wv_ref, wz_ref, wdt_ref, qT_ref, k_ref, vT_ref, z_ref, dt_ref)
    xbc_ref[...] = _dot(u, wx_ref[...])


def _inproj_body(x_ref, xprev_ref, xnext_ref, metaraw_ref, n1_ref, wq_ref, wk_ref, wv_ref, wz_ref,
                 wx_ref, wdt_ref, cw_ref, cb_ref,
                 qT_ref, k_ref, vT_ref, z_ref, act_ref, dt_ref, metaact_ref, *, tm, nt):
    i = pl.program_id(1)
    n1 = n1_ref[...]
    x_ext = jnp.concatenate([xprev_ref[...], x_ref[...], xnext_ref[...]], axis=0)
    u_ext = _rms(x_ext, n1).astype(BF16)
    raw = _dot(u_ext, wx_ref[...])
    w = cw_ref[...]
    b = cb_ref[...]
    prev = jnp.where(i == 0, metaraw_ref[CHUNK - HALO:CHUNK, :], raw[0:HALO, :])
    nxt = jnp.where(i == nt - 1, 0.0, raw[HALO + tm:, :])
    main = raw[HALO:HALO + tm, :]
    act_ref[...] = _conv_taps(jnp.concatenate([prev, main, nxt], axis=0), w, b, tm)
    _project(u_ext[HALO:HALO + tm, :], wq_ref, wk_ref, wv_ref, wz_ref, wdt_ref,
             qT_ref, k_ref, vT_ref, z_ref, dt_ref)

    @pl.when(i == 0)
    def _():
        zeros = jnp.zeros((HALO, CONV_DIM), F32)
        extm = jnp.concatenate([zeros, metaraw_ref[...], main[0:HALO, :]], axis=0)
        am = _conv_taps(extm, w, b, CHUNK)
        row = lax.broadcasted_iota(jnp.int32, (CHUNK, 1), 0)
        metaact_ref[0] = jnp.where(row >= NPAD, am, 0.0)


def _inproj_specs(rows, tm, tile_map):
    out_shape = (
        jax.ShapeDtypeStruct((N_HEADS, HEAD_W, rows), BF16),
        jax.ShapeDtypeStruct((N_HEADS, rows, HEAD_W), BF16),
        jax.ShapeDtypeStruct((N_HEADS, rows // tm, HEAD_W, tm), BF16),
        jax.ShapeDtypeStruct((rows, D_SSM), F32),
        jax.ShapeDtypeStruct((rows, CONV_DIM), F32),
        jax.ShapeDtypeStruct((rows, LANES), F32),
    )
    at = lambda f: (lambda *g: f(tile_map(*g)))
    out_specs = (
        pl.BlockSpec((N_HEADS, HEAD_W, tm), at(lambda r: (0, 0, r))),
        pl.BlockSpec((N_HEADS, tm, HEAD_W), at(lambda r: (0, r, 0))),
        pl.BlockSpec((N_HEADS, 1, HEAD_W, tm), at(lambda r: (0, r, 0, 0))),
        pl.BlockSpec((tm, D_SSM), at(lambda r: (r, 0))),
        pl.BlockSpec((tm, CONV_DIM), at(lambda r: (r, 0))),
        pl.BlockSpec((tm, LANES), at(lambda r: (r, 0))),
    )
    return out_shape, out_specs


def _inproj_meta(meta_block, n1, w_in):
    out_shape, out_specs = _inproj_specs(CHUNK, CHUNK, lambda i: i)
    return pl.pallas_call(
        _inproj_meta_body, out_shape=out_shape, grid=(1,),
        in_specs=[pl.BlockSpec((CHUNK, D_MODEL), lambda i: (i, 0)), _const_spec((1, D_MODEL))]
        + [_const_spec(w.shape) for w in w_in],
        out_specs=out_specs, compiler_params=_params(("arbitrary",)), name="inproj_meta",
    )(meta_block, n1, *w_in)


def _inproj(x2d, meta_raw, n1, w_in, conv_w8, conv_b, nb, seq, tm):
    nt = seq // tm
    hb = tm // HALO
    nblk8 = (nb * seq) // HALO
    out_shape, out_specs = _inproj_specs(nb * seq, tm, lambda b, i: b * nt + i)
    out_shape += (jax.ShapeDtypeStruct((nb, CHUNK, CONV_DIM), F32),)
    out_specs += (pl.BlockSpec((1, CHUNK, CONV_DIM), lambda b, i: (b, 0, 0)),)
    in_specs = [
        pl.BlockSpec((tm, D_MODEL), lambda b, i: (b * nt + i, 0)),
        pl.BlockSpec((HALO, D_MODEL), lambda b, i: (jnp.maximum((b * nt + i) * hb - 1, 0), 0)),
        pl.BlockSpec((HALO, D_MODEL),
                     lambda b, i: (jnp.minimum((b * nt + i + 1) * hb, nblk8 - 1), 0)),
        _const_spec((CHUNK, CONV_DIM)), _const_spec((1, D_MODEL)),
    ] + [_const_spec(w.shape) for w in w_in] + [_const_spec((HALO, CONV_DIM)),
                                                 _const_spec((1, CONV_DIM))]
    return pl.pallas_call(
        functools.partial(_inproj_body, tm=tm, nt=nt), out_shape=out_shape, grid=(nb, nt),
        in_specs=in_specs, out_specs=out_specs,
        compiler_params=_params(("arbitrary", "arbitrary")), name="inproj",
    )(x2d, x2d, x2d, meta_raw, n1, *w_in, conv_w8, conv_b)


def _split3(x):
    hi = x.astype(BF16)
    r = x - hi.astype(F32)
    mid = r.astype(BF16)
    lo = (r - mid.astype(F32)).astype(BF16)
    return hi, mid, lo


def _expand_heads(x, e2):
    hi = x.astype(BF16)
    lo = (x - hi.astype(F32)).astype(BF16)
    return _dot(jnp.concatenate([hi, lo], axis=1), e2)


def _ssd_chunk(act, dtr, valid, tri, e, biasv, alogv, lane_lo, st_ref, fwd, want_y=True):
    xs = act[:, :D_SSM]
    lane = lax.broadcasted_iota(jnp.int32, (1, LANES), 1)
    hmask = (lane >= lane_lo) & (lane < lane_lo + H_SSM)
    x = dtr + biasv
    dt = jnp.maximum(x, 0.0) + jnp.log(1.0 + jnp.exp(-jnp.abs(x)))
    dt = jnp.where(hmask & valid, dt, 0.0)
    adt = dt * (-jnp.exp(alogv))
    hi, mid, lo = _split3(adt)
    acs = (_dot(jnp.concatenate([tri, tri], axis=1), jnp.concatenate([hi, mid], axis=0))
           + _dot(tri, lo))
    acs_t = acs.T
    edge = acs[CHUNK - 1:CHUNK, :] if fwd else acs[0:1, :]
    e2 = jnp.concatenate([e, e], axis=0)
    expanded = _expand_heads(jnp.concatenate([dt, jnp.exp(acs), jnp.exp(edge - acs)], axis=0), e2)
    dt_x = expanded[0:CHUNK, :]
    ein_x = expanded[CHUNK:2 * CHUNK, :]
    est_x = expanded[2 * CHUNK:3 * CHUNK, :]
    cd_x = ein_x[CHUNK - 1:CHUNK, :] if fwd else ein_x[0:1, :]
    xg = xs * dt_x
    xgd = (xg * est_x).astype(BF16)
    li = lax.broadcasted_iota(jnp.int32, (CHUNK, CHUNK), 0)
    si = lax.broadcasted_iota(jnp.int32, (CHUNK, CHUNK), 1)
    tmask = (li >= si) if fwd else (si >= li)
    left = lax.broadcasted_iota(jnp.int32, (1, LANES), 1) < SSM_HD
    ys = []
    heads_per_group = H_SSM // SSM_G
    for g in range(SSM_G):
        bg = act[:, D_SSM + g * SSM_N:D_SSM + (g + 1) * SSM_N]
        cg = act[:, D_SSM + SSM_G * SSM_N + g * SSM_N:D_SSM + SSM_G * SSM_N + (g + 1) * SSM_N]
        cb = cg.astype(BF16)
        bb = bg.astype(BF16)
        st = st_ref[g]
        gsl = slice(g * heads_per_group * SSM_HD, (g + 1) * heads_per_group * SSM_HD)
        st_ref[g] = st * cd_x[:, gsl] + _dot(bg.T.astype(BF16), xgd[:, gsl])
        if not want_y:
            continue
        gm = lax.dot_general(cb, bb, (((1,), (1,)), ((), ())), preferred_element_type=F32)
        yoff = _dot(cb, st.astype(BF16))
        for jp in range(heads_per_group // 2):
            pair = g * (heads_per_group // 2) + jp
            sl = slice(pair * LANES, (pair + 1) * LANES)
            xg_pair = xg[:, sl]
            mds = []
            for hh in range(2):
                c = lane_lo + pair * 2 + hh
                seg = acs[:, c:c + 1] - acs_t[c:c + 1, :]
                decay = jnp.exp(jnp.where(tmask, seg, NEG))
                mds.append((gm * decay).astype(BF16))
            xg2 = jnp.concatenate([jnp.where(left, xg_pair, 0.0).astype(BF16),
                                   jnp.where(left, 0.0, xg_pair).astype(BF16)], axis=0)
            ys.append(ein_x[:, sl] * yoff[:, jp * LANES:(jp + 1) * LANES]
                      + _dot(jnp.concatenate(mds, axis=1), xg2))
    return (jnp.concatenate(ys, axis=1) if want_y else None), xs


def _ssd_fwd_body(actx_ref, actm_ref, dtx_ref, dtm_ref, tri_ref, e_ref, biasv_ref, alogv_ref,
                  yf_ref, st_ref):
    consts = (tri_ref[...], e_ref[...], biasv_ref[...], alogv_ref[...], 0, st_ref, True)

    @pl.when(pl.program_id(1) == 0)
    def _():
        st_ref[...] = jnp.zeros_like(st_ref)
        row = lax.broadcasted_iota(jnp.int32, (CHUNK, 1), 0)
        _ssd_chunk(actm_ref[0], dtm_ref[...], row >= NPAD, *consts, want_y=False)

    for sub in range(SSD_CHUNKS_PER_STEP):
        rows = slice(sub * CHUNK, (sub + 1) * CHUNK)
        y, _ = _ssd_chunk(actx_ref[rows, :], dtx_ref[rows, :], True, *consts)
        yf_ref[rows, :] = y


def _ssd_bwd_body(act_ref, dt_ref, yf_ref, z_ref, tri_ref, e_ref, biasv_ref, alogv_ref,
                  dskip_ref, nw_ref, out_ref, st_ref):
    @pl.when(pl.program_id(1) == 0)
    def _():
        st_ref[...] = jnp.zeros_like(st_ref)

    for sub in reversed(range(SSD_CHUNKS_PER_STEP)):
        rows = slice(sub * CHUNK, (sub + 1) * CHUNK)
        yb, xs = _ssd_chunk(act_ref[rows, :], dt_ref[rows, :], True, tri_ref[...], e_ref[...],
                            biasv_ref[...], alogv_ref[...], H_SSM, st_ref, False)
        y = yf_ref[rows, :] + yb + xs * dskip_ref[...]
        out_ref[rows, :] = _rms(y * _silu(z_ref[rows, :]), nw_ref[...]).astype(BF16)


def _ssd(act, meta_act, dt_raw, dt_meta, z, consts, nb, seq):
    step = SSD_CHUNKS_PER_STEP * CHUNK
    ns = seq // step
    tril, triu, e_f, e_b, biasv, alogv, dskip_x, ssm_nw = consts
    st_shape = pltpu.VMEM((SSM_G, SSM_N, (H_SSM // SSM_G) * SSM_HD), F32)

    xblk = lambda b, c: (b * ns + c, 0)
    y_f = pl.pallas_call(
        _ssd_fwd_body,
        out_shape=jax.ShapeDtypeStruct((nb * seq, D_SSM), F32),
        grid=(nb, ns),
        in_specs=[
            pl.BlockSpec((step, CONV_DIM), xblk),
            pl.BlockSpec((1, CHUNK, CONV_DIM), lambda b, c: (b, 0, 0)),
            pl.BlockSpec((step, LANES), xblk),
            _const_spec((CHUNK, LANES)),
            _const_spec((CHUNK, CHUNK)), _const_spec((LANES, D_SSM)),
            _const_spec((1, LANES)), _const_spec((1, LANES)),
        ],
        out_specs=pl.BlockSpec((step, D_SSM), xblk),
        scratch_shapes=[st_shape],
        compiler_params=_params(("arbitrary", "arbitrary")), name="ssd_fwd",
    )(act, meta_act, dt_raw, dt_meta, tril, e_f, biasv, alogv)

    rblk = lambda b, c: (b * ns + ns - 1 - c, 0)
    return pl.pallas_call(
        _ssd_bwd_body,
        out_shape=jax.ShapeDtypeStruct((nb * seq, D_SSM), BF16),
        grid=(nb, ns),
        in_specs=[
            pl.BlockSpec((step, CONV_DIM), rblk),
            pl.BlockSpec((step, LANES), rblk),
            pl.BlockSpec((step, D_SSM), rblk),
            pl.BlockSpec((step, D_SSM), rblk),
            _const_spec((CHUNK, CHUNK)), _const_spec((LANES, D_SSM)),
            _const_spec((1, LANES)), _const_spec((1, LANES)),
            _const_spec((1, D_SSM)), _const_spec((1, D_SSM)),
        ],
        out_specs=pl.BlockSpec((step, D_SSM), rblk),
        scratch_shapes=[st_shape],
        compiler_params=_params(("arbitrary", "arbitrary")), name="ssd_bwd",
    )(act, dt_raw, y_f, z, triu, e_b, biasv, alogv, dskip_x, ssm_nw)


def _attn_body(qT_ref, k_ref, vT_ref, km_ref, vmT_ref, slope_ref, lq1_ref, lk1_ref, lq2_ref,
               lk2_ref, nw_ref, out_ref, qaug_ref, sa_ref, sb_ref, acc_ref, dbias_ref, feat_ref,
               kfeat_ref,
               *, t, seq, unrolled):
    nkv = seq // t
    slope = slope_ref[0]
    slope_q = slope[:, 0:1]
    row = lax.broadcasted_iota(jnp.int32, (HEAD_W, 1), 0)

    def log2e_part(idx):
        part = idx % 3
        return jnp.where(part == 0, LOG2E_PARTS[0],
                         jnp.where(part == 1, LOG2E_PARTS[1], LOG2E_PARTS[2]))

    def key_feat(rows, rel, valid):
        lane = lax.broadcasted_iota(jnp.int32, (rows, LANES), 1)
        r_hi = (rel // BF16_EXACT_INTS) * BF16_EXACT_INTS
        f = jnp.where(lane < 3, slope * r_hi.astype(F32),
                      jnp.where(lane < 6, slope * (rel - r_hi).astype(F32),
                                jnp.where(lane < 12, log2e_part(lane), 0.0)))
        if valid is not None:
            f = jnp.where(valid, f, jnp.where(lane == 3, NEG, 0.0))
        return f.astype(BF16)

    @pl.when(jnp.logical_and(pl.program_id(1) == 0, pl.program_id(2) == 0))
    def _():
        frow = lax.broadcasted_iota(jnp.int32, (HEAD_W, t), 0)
        fcol = lax.broadcasted_iota(jnp.int32, (HEAD_W, t), 1)
        c_hi = (fcol // BF16_EXACT_INTS) * BF16_EXACT_INTS
        feat = jnp.where(frow < 6, log2e_part(frow),
                         jnp.where(frow < 9, -slope_q * c_hi.astype(F32),
                                   jnp.where(frow < 12, -slope_q * (fcol - c_hi).astype(F32), 0.0)))
        feat_ref[0] = feat.astype(BF16)
        feat_ref[1] = (-feat).astype(BF16)
        kfeat_ref[0:t, :] = key_feat(t, lax.broadcasted_iota(jnp.int32, (t, LANES), 0), None)
        mrow = lax.broadcasted_iota(jnp.int32, (CHUNK, LANES), 0)
        kfeat_ref[t:t + CHUNK, :] = key_feat(CHUNK, mrow - CHUNK, mrow >= NPAD)
        d = (lax.broadcasted_iota(jnp.int32, (t, t), 0)
             - lax.broadcasted_iota(jnp.int32, (t, t), 1))
        dbias_ref[...] = jnp.where(d > 0, (-2.0 * LOG2E * slope_q) * d.astype(F32), 0.0)

    def set_queries(qT, slot):
        zero = jnp.zeros_like(qT)
        qz = (jnp.where(row < ATT_DH, qT, zero), jnp.where(row >= ATT_DH, qT, zero))
        for sg in range(2):
            for mp in range(2):
                qaug_ref[slot, sg, mp] = jnp.concatenate([qz[mp], feat_ref[sg]], axis=0)

    ones_rows = jnp.ones((ONES_ROWS, t), BF16)
    ones_m = jnp.ones((ONES_ROWS, CHUNK), BF16)
    lam = (jnp.exp(jnp.sum(lq1_ref[...] * lk1_ref[...], axis=-1, keepdims=True))
           - jnp.exp(jnp.sum(lq2_ref[...] * lk2_ref[...], axis=-1, keepdims=True)) + LAM_INIT)
    slots = (sa_ref, sb_ref)

    def as_f32(x):
        return float(x) if isinstance(x, int) else x.astype(F32)

    def x_lhs(j):
        k0 = j * t if isinstance(j, int) else pl.multiple_of(j * t, t)
        return jnp.concatenate([k_ref[0, pl.ds(k0, t), :], kfeat_ref[0:t, :]], axis=1)

    def unit_shift(unit, i):
        if unit[0] == "x":
            d = unit[1] - i
            return (LOG2E * slope_q) * as_f32(-(abs(d) if isinstance(d, int) else jnp.abs(d)) * t)
        return (LOG2E * slope_q) * as_f32(-(i * t)) if unit[0] == "meta" else 0.0

    def unit_rows(unit):
        return CHUNK if unit[0] == "meta" else t

    def produce(unit, i, base, qslot, s_ref):
        if unit[0] == "x":
            j = unit[1]
            after = j > i
            sg = int(after) if isinstance(after, bool) else after.astype(jnp.int32)
            lhs = x_lhs(base + j)
            blocks = [_dot(lhs, qaug_ref[qslot, sg, mp]) for mp in range(2)]
        elif unit[0] == "meta":
            lhs = jnp.concatenate([km_ref[0], kfeat_ref[t:t + CHUNK, :]], axis=1)
            blocks = [_dot(lhs, qaug_ref[qslot, 0, mp]) for mp in range(2)]
        else:
            lhs = x_lhs(base + i)
            blocks = [_dot(lhs, qaug_ref[qslot, 0, mp]) + dbias_ref[...] for mp in range(2)]
        rows = unit_rows(unit)
        mbs = []
        for mp, s in enumerate(blocks):
            s_ref[mp, 0:rows, :] = s
            mbs.append(jnp.max(s, axis=0, keepdims=True))
        return tuple(mbs)

    def consume(unit, i, base, s_ref, mbs, ms):
        cterm = unit_shift(unit, i)
        rows = unit_rows(unit)
        if unit[0] == "meta":
            v_aug = jnp.concatenate([vmT_ref[0, 0], ones_m], axis=0)
        else:
            j = unit[1] if unit[0] == "x" else i
            v_aug = jnp.concatenate([vT_ref[0, base + j], ones_rows], axis=0)
        out = []
        for mp in range(2):
            m_new = jnp.maximum(ms[mp], mbs[mp] + cterm)
            p = jnp.exp2(s_ref[mp, 0:rows, :] - (m_new - cterm)).astype(BF16)
            alpha = jnp.exp2(ms[mp] - m_new)
            acc_ref[mp] = acc_ref[mp] * alpha + _dot(v_aug, p)
            out.append(m_new)
        return tuple(out)

    def start_tile():
        acc_ref[...] = jnp.zeros_like(acc_ref)
        return (jnp.full((1, t), NEG, F32),) * 2

    def finalize(rows):
        o0 = acc_ref[0, 0:HEAD_W, :] / acc_ref[0, HEAD_W:HEAD_W + 1, :]
        o1 = acc_ref[1, 0:HEAD_W, :] / acc_ref[1, HEAD_W:HEAD_W + 1, :]
        o = o0 - lam * o1
        on = o * lax.rsqrt(jnp.mean(o * o, axis=0, keepdims=True) + EPS)
        out_ref[rows, :] = ((on.T * nw_ref[...]) * (1.0 - LAM_INIT)).astype(BF16)

    def run_static(units, pending, ms):
        for u, (l, i, base, unit) in enumerate(units):
            nxt = None
            if u + 1 < len(units):
                nl, ni, nbase, nunit = units[u + 1]
                if nl != l:
                    set_queries(qT_ref[0, :, nl * t:(nl + 1) * t], nl % 2)
                nxt = produce(nunit, ni, nbase, nl % 2, slots[(u + 1) % 2])
            if unrolled and (u == 0 or units[u - 1][0] != l):
                ms = start_tile()
            ms = consume(unit, i, base, slots[u % 2], pending, ms)
            pending = nxt
            if unrolled and (u + 1 == len(units) or units[u + 1][0] != l):
                finalize(slice(l * t, (l + 1) * t))
        return ms

    def x_units(i):
        return [("x", n + (n >= i)) for n in range(nkv - 1)]

    tail = [("diag",), ("meta",)]
    if unrolled:
        units = []
        for l in range(unrolled):
            i, base = l % nkv, (l // nkv) * nkv
            units += [(l, i, base, u) for u in x_units(i) + tail]
        set_queries(qT_ref[0, :, 0:t], 0)
        run_static(units, produce(units[0][3], 0, 0, 0, sa_ref), None)
    else:
        i = pl.program_id(2)
        set_queries(qT_ref[0], 0)
        ms = start_tile()
        n_off = nkv - 1
        xunit = lambda n: ("x", n + (n >= i).astype(jnp.int32))
        rest = tail
        if n_off >= 1:
            mbs = produce(xunit(0), i, 0, 0, sa_ref)
            pairs = (n_off - 1) // 2

            def pair_body(u, carry):
                ms, mbs = carry[:2], carry[2:]
                n = 2 * u
                nxt = produce(xunit(n + 1), i, 0, 0, sb_ref)
                ms = consume(xunit(n), i, 0, sa_ref, mbs, ms)
                mbs = produce(xunit(n + 2), i, 0, 0, sa_ref)
                ms = consume(xunit(n + 1), i, 0, sb_ref, nxt, ms)
                return ms + mbs

            unroll = next((f for f in (5, 3) if pairs % f == 0), 1)
            carry = lax.fori_loop(0, pairs, pair_body, ms + mbs, unroll=unroll)
            ms, mbs = carry[:2], carry[2:]
            rest = [xunit(n) for n in range(2 * pairs, n_off)] + tail
        else:
            mbs = produce(rest[0], i, 0, 0, sa_ref)
        run_static([(0, i, 0, u) for u in rest], mbs, ms)
        finalize(slice(None))


def _attention(qT, k, vT, k_meta, vT_meta, slopes, lq1, lk1, lq2, lk2, attn_nw, nb, seq):
    t = min(ROW_TILE, seq)
    nq = seq // t
    unrolled, group = 0, 1
    if nq * (nq + 1) <= ATTN_UNROLL_MAX_UNITS:
        group = ATTN_UNROLL_SEQS if nb % ATTN_UNROLL_SEQS == 0 else 1
        unrolled = group * nq
    tq = max(unrolled, 1) * t
    steps = (group * seq) // tq
    grid = (N_HEADS, nb // group, steps)
    q_map = lambda h, b, i: (h, 0, b * steps + i)
    out_map = lambda h, b, i: (b * steps + i, h)
    per_seq = lambda f: (lambda h, b, i: f(h, b))
    vec = lambda n: _const_spec((1, n))
    in_specs = [
        pl.BlockSpec((1, HEAD_W, tq), q_map),
        pl.BlockSpec((1, group * seq, HEAD_W), per_seq(lambda h, b: (h, b, 0))),
        pl.BlockSpec((1, group * nq, HEAD_W, t), per_seq(lambda h, b: (h, b, 0, 0))),
        pl.BlockSpec((1, CHUNK, HEAD_W), per_seq(lambda h, b: (h, 0, 0))),
        pl.BlockSpec((1, 1, HEAD_W, CHUNK), per_seq(lambda h, b: (h, 0, 0, 0))),
        pl.BlockSpec((1, 1, LANES), per_seq(lambda h, b: (h, 0, 0))),
        vec(ATT_DH), vec(ATT_DH), vec(ATT_DH), vec(ATT_DH), vec(HEAD_W),
    ]
    return pl.pallas_call(
        functools.partial(_attn_body, t=t, seq=seq, unrolled=unrolled),
        out_shape=jax.ShapeDtypeStruct((nb * seq, ATT_VDIM), BF16),
        grid=grid, in_specs=in_specs,
        out_specs=pl.BlockSpec((tq, HEAD_W), out_map),
        scratch_shapes=[pltpu.VMEM((2, 2, 2, 2 * HEAD_W, t), BF16),
                        pltpu.VMEM((2, t, t), F32), pltpu.VMEM((2, t, t), F32),
                        pltpu.VMEM((2, HEAD_W + ONES_ROWS, t), F32),
                        pltpu.VMEM((t, t), F32),
                        pltpu.VMEM((2, HEAD_W, t), BF16),
                        pltpu.VMEM((t + CHUNK, LANES), BF16)],
        compiler_params=_params(("arbitrary",) * len(grid)), name="attn",
    )(qT, k, vT, k_meta, vT_meta, slopes, lq1, lk1, lq2, lk2, attn_nw)


def _mlp_body(x_ref, att_ref, ssm_ref, wo_ref, n2_ref, wup_ref, wdown_ref, fw_ref, out_ref):
    h1 = (x_ref[...] + _dot(att_ref[...], wo_ref[0:ATT_VDIM, :])
          + _dot(ssm_ref[...], wo_ref[ATT_VDIM:ATT_VDIM + D_SSM, :]))
    u2 = _rms(h1, n2_ref[...]).astype(BF16)
    hid = jnp.maximum(_dot(u2, wup_ref[...]), 0.0)
    h2 = h1 + _dot((hid * hid).astype(BF16), wdown_ref[...])
    out_ref[...] = _rms(h2, fw_ref[...])


def _mlp(x2d, att, ssm, wo, n2, wup, wdown, fw, tm):
    rows = x2d.shape[0]
    tile = lambda w: pl.BlockSpec((tm, w), lambda i: (i, 0))
    return pl.pallas_call(
        _mlp_body, out_shape=jax.ShapeDtypeStruct((rows, D_MODEL), F32), grid=(rows // tm,),
        in_specs=[tile(D_MODEL), tile(ATT_VDIM), tile(D_SSM), _const_spec(wo.shape),
                  _const_spec((1, D_MODEL)), _const_spec(wup.shape), _const_spec(wdown.shape),
                  _const_spec((1, D_MODEL))],
        out_specs=tile(D_MODEL),
        compiler_params=_params(("arbitrary",)), name="mlp",
    )(x2d, att, ssm, wo, n2, wup, wdown, fw)


def _head_lane_vec(fwd_vals, bwd_vals):
    v = jnp.zeros((1, LANES), F32)
    v = v.at[0, 0:H_SSM].set(fwd_vals.astype(F32))
    return v.at[0, H_SSM:2 * H_SSM].set(bwd_vals.astype(F32))


def _expand_matrix(lane_lo):
    lane = jnp.arange(LANES)[:, None]
    ch = jnp.arange(D_SSM)[None, :] // SSM_HD
    return (lane == ch + lane_lo).astype(BF16)


def _trunk(x, meta_parts, p):
    nb, seq, _ = x.shape
    x2d = x.reshape(nb * seq, D_MODEL)
    k_meta, vT_meta, xbc_meta, dt_meta = meta_parts
    tm = min(ROW_TILE, seq)
    qT, k, vT, z, act, dt_raw, meta_act = _inproj(
        x2d, xbc_meta, p["n1"], p["w_in"], p["conv_w8"], p["conv_b"], nb, seq, tm)
    ssm = _ssd(act, meta_act, dt_raw, dt_meta, z, p["ssd_consts"], nb, seq)
    att = _attention(qT, k, vT, k_meta, vT_meta, p["slopes"], p["lq1"], p["lk1"], p["lq2"],
                     p["lk2"], p["attn_nw"], nb, seq)
    y = _mlp(x2d, att, ssm, p["wo"], p["n2"], p["wup"], p["wdown"], p["fw"], tm)
    return y.reshape(nb, seq, D_MODEL)


@jax.jit
def kernel(x_prompt, x_sample, meta_tokens, norm1_w, w_in, conv_w, conv_b, lambda_q1, lambda_k1,
           lambda_q2, lambda_k2, attn_norm_w, dt_bias_f, dt_bias_b, a_log_f, a_log_b, d_skip,
           ssm_norm_w, w_out, norm2_w, w_up, w_down, final_norm_w):
    li = 0
    w = w_in[li].astype(BF16)
    cuts = (0, ATT_VDIM, 2 * ATT_VDIM, 3 * ATT_VDIM, 3 * ATT_VDIM + D_SSM,
            3 * ATT_VDIM + D_SSM + CONV_DIM)
    wdt = jnp.zeros((D_MODEL, LANES), BF16).at[:, 0:2 * H_SSM].set(w[:, cuts[5]:])
    t = jnp.arange(CHUNK)
    p = {
        "n1": norm1_w[li].reshape(1, D_MODEL).astype(F32),
        "w_in": tuple(w[:, cuts[j]:cuts[j + 1]] for j in range(5)) + (wdt,),
        "conv_w8": jnp.zeros((HALO, CONV_DIM), F32).at[0:D_CONV].set(conv_w[li].astype(F32)),
        "conv_b": conv_b[li].reshape(1, CONV_DIM).astype(F32),
        "ssd_consts": (
            (t[:, None] >= t[None, :]).astype(BF16),
            (t[None, :] >= t[:, None]).astype(BF16),
            _expand_matrix(0), _expand_matrix(H_SSM),
            _head_lane_vec(dt_bias_f[li], dt_bias_b[li]),
            _head_lane_vec(a_log_f[li], a_log_b[li]),
            jnp.repeat(d_skip[li].astype(F32), SSM_HD).reshape(1, D_SSM),
            ssm_norm_w[li].reshape(1, D_SSM).astype(F32),
        ),
        "slopes": jnp.broadcast_to(
            (2.0 ** (-8.0 * (jnp.arange(N_HEADS, dtype=F32) + 1.0) / N_HEADS))[:, None, None],
            (N_HEADS, 1, LANES)),
        "lq1": lambda_q1[li].reshape(1, ATT_DH).astype(F32),
        "lk1": lambda_k1[li].reshape(1, ATT_DH).astype(F32),
        "lq2": lambda_q2[li].reshape(1, ATT_DH).astype(F32),
        "lk2": lambda_k2[li].reshape(1, ATT_DH).astype(F32),
        "attn_nw": attn_norm_w[li].reshape(1, HEAD_W).astype(F32),
        "wo": w_out[li].astype(BF16),
        "n2": norm2_w[li].reshape(1, D_MODEL).astype(F32),
        "wup": w_up[li].astype(BF16),
        "wdown": w_down[li].astype(BF16),
        "fw": final_norm_w.reshape(1, D_MODEL).astype(F32),
    }
    meta_block = jnp.concatenate(
        [jnp.zeros((NPAD, D_MODEL), F32), meta_tokens.astype(F32)], axis=0)
    _, k_m, vT_m, _, xbc_m, dt_m = _inproj_meta(meta_block, p["n1"], p["w_in"])
    meta_parts = (k_m, vT_m, xbc_m, dt_m)
    return (_trunk(x_prompt, meta_parts, p), _trunk(x_sample, meta_parts, p))
```

```python
import functools
import math

import jax
import jax.numpy as jnp
import numpy as np
from jax import lax
from jax.experimental import pallas as pl
from jax.experimental.pallas import tpu as pltpu

F32 = jnp.float32
BF16 = jnp.bfloat16

D_MODEL = 1024
N_META = 16
CHUNK = 128
NPAD = CHUNK - N_META
N_HEADS = 8
ATT_DH = 64
HEAD_W = 2 * ATT_DH
ATT_VDIM = N_HEADS * HEAD_W
D_SSM = 1024
H_SSM = 16
SSM_HD = 64
SSM_G = 2
SSM_N = 128
D_CONV = 5
CONV_DIM = D_SSM + 2 * SSM_G * SSM_N
D_FF = 4 * D_MODEL
EPS = 1e-5
LAM_INIT = 0.8 - 0.6 * math.exp(-0.3 * 0)
NEG = -1e30
LOG2E = math.log2(math.e)
BF16_EXACT_INTS = 256


def _bf16_parts(x, n):
    parts, r = [], np.float32(x)
    for _ in range(n):
        h = np.float32(np.asarray(r, dtype=jnp.bfloat16))
        parts.append(float(h))
        r = np.float32(r - h)
    return tuple(parts)


LOG2E_PARTS = _bf16_parts(LOG2E, 3)

LANES = 128
HALO = 8
VMEM_LIMIT = 56 * 1024 * 1024

ROW_TILE = 512
ONES_ROWS = 16
SSD_CHUNKS_PER_STEP = 4
ATTN_UNROLL_MAX_UNITS = 24
ATTN_UNROLL_SEQS = 2


def _dot(a, b):
    return jnp.dot(a, b, preferred_element_type=F32)


def _rms(x, w):
    return x * lax.rsqrt(jnp.mean(x * x, axis=-1, keepdims=True) + EPS) * w


def _silu(x):
    return x / (1.0 + jnp.exp(-x))


def _const_spec(shape):
    nd = len(shape)
    return pl.BlockSpec(shape, lambda *_: (0,) * nd, pipeline_mode=pl.Buffered(1))


def _params(sem, flags=None):
    return pltpu.CompilerParams(dimension_semantics=sem, vmem_limit_bytes=VMEM_LIMIT, flags=flags)


def _conv_taps(ext, w, b, rows):
    n = ext.shape[0]
    acc = b
    for j in range(D_CONV):
        shifted = ext if j == 2 else pltpu.roll(ext, (2 - j) % n, 0)
        acc = acc + w[j:j + 1, :] * shifted[HALO:HALO + rows, :]
    return _silu(acc)


def _project(u, wq_ref, wk_ref, wv_ref, wz_ref, wdt_ref, qT_ref, k_ref, vT_ref, z_ref, dt_ref):
    q = _dot(u, wq_ref[...]) * (ATT_DH ** -0.5 * LOG2E)
    for h in range(N_HEADS):
        qT_ref[h] = q[:, h * HEAD_W:(h + 1) * HEAD_W].T.astype(BF16)
    k = _dot(u, wk_ref[...])
    for h in range(N_HEADS):
        k_ref[h] = k[:, h * HEAD_W:(h + 1) * HEAD_W].astype(BF16)
    v = _dot(u, wv_ref[...])
    for h in range(N_HEADS):
        vT_ref[h, 0] = v[:, h * HEAD_W:(h + 1) * HEAD_W].T.astype(BF16)
    z_ref[...] = _dot(u, wz_ref[...])
    dt_ref[...] = _dot(u, wdt_ref[...])


def _inproj_meta_body(x_ref, n1_ref, wq_ref, wk_ref, wv_ref, wz_ref, wx_ref, wdt_ref,
                      qT_ref, k_ref, vT_ref, z_ref, xbc_ref, dt_ref):
    u = _rms(x_ref[...], n1_ref[...]).astype(BF16)
    _project(u, wq_ref, wk_ref, wv_ref, wz_ref, wdt_ref, qT_ref, k_ref, vT_ref, z_ref, dt_ref)
    xbc_ref[...] = _dot(u, wx_ref[...])


def _inproj_body(x_ref, xprev_ref, xnext_ref, metaraw_ref, n1_ref, wq_ref, wk_ref, wv_ref, wz_ref,
                 wx_ref, wdt_ref, cw_ref, cb_ref,
                 qT_ref, k_ref, vT_ref, z_ref, act_ref, dt_ref, metaact_ref, *, tm, nt):
    i = pl.program_id(1)
    n1 = n1_ref[...]
    x_ext = jnp.concatenate([xprev_ref[...], x_ref[...], xnext_ref[...]], axis=0)
    u_ext = _rms(x_ext, n1).astype(BF16)
    raw = _dot(u_ext, wx_ref[...])
    w = cw_ref[...]
    b = cb_ref[...]
    prev = jnp.where(i == 0, metaraw_ref[CHUNK - HALO:CHUNK, :], raw[0:HALO, :])
    nxt = jnp.where(i == nt - 1, 0.0, raw[HALO + tm:, :])
    main = raw[HALO:HALO + tm, :]
    act_ref[...] = _conv_taps(jnp.concatenate([prev, main, nxt], axis=0), w, b, tm)
    _project(u_ext[HALO:HALO + tm, :], wq_ref, wk_ref, wv_ref, wz_ref, wdt_ref,
             qT_ref, k_ref, vT_ref, z_ref, dt_ref)

    @pl.when(i == 0)
    def _():
        zeros = jnp.zeros((HALO, CONV_DIM), F32)
        extm = jnp.concatenate([zeros, metaraw_ref[...], main[0:HALO, :]], axis=0)
        am = _conv_taps(extm, w, b, CHUNK)
        row = lax.broadcasted_iota(jnp.int32, (CHUNK, 1), 0)
        metaact_ref[0] = jnp.where(row >= NPAD, am, 0.0)


def _inproj_specs(rows, tm, tile_map):
    out_shape = (
        jax.ShapeDtypeStruct((N_HEADS, HEAD_W, rows), BF16),
        jax.ShapeDtypeStruct((N_HEADS, rows, HEAD_W), BF16),
        jax.ShapeDtypeStruct((N_HEADS, rows // tm, HEAD_W, tm), BF16),
        jax.ShapeDtypeStruct((rows, D_SSM), F32),
        jax.ShapeDtypeStruct((rows, CONV_DIM), F32),
        jax.ShapeDtypeStruct((rows, LANES), F32),
    )
    at = lambda f: (lambda *g: f(tile_map(*g)))
    out_specs = (
        pl.BlockSpec((N_HEADS, HEAD_W, tm), at(lambda r: (0, 0, r))),
        pl.BlockSpec((N_HEADS, tm, HEAD_W), at(lambda r: (0, r, 0))),
        pl.BlockSpec((N_HEADS, 1, HEAD_W, tm), at(lambda r: (0, r, 0, 0))),
        pl.BlockSpec((tm, D_SSM), at(lambda r: (r, 0))),
        pl.BlockSpec((tm, CONV_DIM), at(lambda r: (r, 0))),
        pl.BlockSpec((tm, LANES), at(lambda r: (r, 0))),
    )
    return out_shape, out_specs


def _inproj_meta(meta_block, n1, w_in):
    out_shape, out_specs = _inproj_specs(CHUNK, CHUNK, lambda i: i)
    return pl.pallas_call(
        _inproj_meta_body, out_shape=out_shape, grid=(1,),
        in_specs=[pl.BlockSpec((CHUNK, D_MODEL), lambda i: (i, 0)), _const_spec((1, D_MODEL))]
        + [_const_spec(w.shape) for w in w_in],
        out_specs=out_specs, compiler_params=_params(("arbitrary",)), name="inproj_meta",
    )(meta_block, n1, *w_in)


def _inproj(x2d, meta_raw, n1, w_in, conv_w8, conv_b, nb, seq, tm):
    nt = seq // tm
    hb = tm // HALO
    nblk8 = (nb * seq) // HALO
    out_shape, out_specs = _inproj_specs(nb * seq, tm, lambda b, i: b * nt + i)
    out_shape += (jax.ShapeDtypeStruct((nb, CHUNK, CONV_DIM), F32),)
    out_specs += (pl.BlockSpec((1, CHUNK, CONV_DIM), lambda b, i: (b, 0, 0)),)
    in_specs = [
        pl.BlockSpec((tm, D_MODEL), lambda b, i: (b * nt + i, 0)),
        pl.BlockSpec((HALO, D_MODEL), lambda b, i: (jnp.maximum((b * nt + i) * hb - 1, 0), 0)),
        pl.BlockSpec((HALO, D_MODEL),
                     lambda b, i: (jnp.minimum((b * nt + i + 1) * hb, nblk8 - 1), 0)),
        _const_spec((CHUNK, CONV_DIM)), _const_spec((1, D_MODEL)),
    ] + [_const_spec(w.shape) for w in w_in] + [_const_spec((HALO, CONV_DIM)),
                                                 _const_spec((1, CONV_DIM))]
    return pl.pallas_call(
        functools.partial(_inproj_body, tm=tm, nt=nt), out_shape=out_shape, grid=(nb, nt),
        in_specs=in_specs, out_specs=out_specs,
        compiler_params=_params(("arbitrary", "arbitrary")), name="inproj",
    )(x2d, x2d, x2d, meta_raw, n1, *w_in, conv_w8, conv_b)


def _split3(x):
    hi = x.astype(BF16)
    r = x - hi.astype(F32)
    mid = r.astype(BF16)
    lo = (r - mid.astype(F32)).astype(BF16)
    return hi, mid, lo


def _expand_heads(x, e2):
    hi = x.astype(BF16)
    lo = (x - hi.astype(F32)).astype(BF16)
    return _dot(jnp.concatenate([hi, lo], axis=1), e2)


def _ssd_chunk(act, dtr, valid, tri, e, biasv, alogv, lane_lo, st_ref, fwd, want_y=True):
    xs = act[:, :D_SSM]
    lane = lax.broadcasted_iota(jnp.int32, (1, LANES), 1)
    hmask = (lane >= lane_lo) & (lane < lane_lo + H_SSM)
    x = dtr + biasv
    dt = jnp.maximum(x, 0.0) + jnp.log(1.0 + jnp.exp(-jnp.abs(x)))
    dt = jnp.where(hmask & valid, dt, 0.0)
    adt = dt * (-jnp.exp(alogv))
    hi, mid, lo = _split3(adt)
    acs = (_dot(jnp.concatenate([tri, tri], axis=1), jnp.concatenate([hi, mid], axis=0))
           + _dot(tri, lo))
    acs_t = acs.T
    edge = acs[CHUNK - 1:CHUNK, :] if fwd else acs[0:1, :]
    e2 = jnp.concatenate([e, e], axis=0)
    expanded = _expand_heads(jnp.concatenate([dt, jnp.exp(acs), jnp.exp(edge - acs)], axis=0), e2)
    dt_x = expanded[0:CHUNK, :]
    ein_x = expanded[CHUNK:2 * CHUNK, :]
    est_x = expanded[2 * CHUNK:3 * CHUNK, :]
    cd_x = ein_x[CHUNK - 1:CHUNK, :] if fwd else ein_x[0:1, :]
    xg = xs * dt_x
    xgd = (xg * est_x).astype(BF16)
    li = lax.broadcasted_iota(jnp.int32, (CHUNK, CHUNK), 0)
    si = lax.broadcasted_iota(jnp.int32, (CHUNK, CHUNK), 1)
    tmask = (li >= si) if fwd else (si >= li)
    left = lax.broadcasted_iota(jnp.int32, (1, LANES), 1) < SSM_HD
    ys = []
    heads_per_group = H_SSM // SSM_G
    for g in range(SSM_G):
        bg = act[:, D_SSM + g * SSM_N:D_SSM + (g + 1) * SSM_N]
        cg = act[:, D_SSM + SSM_G * SSM_N + g * SSM_N:D_SSM + SSM_G * SSM_N + (g + 1) * SSM_N]
        cb = cg.astype(BF16)
        bb = bg.astype(BF16)
        st = st_ref[g]
        gsl = slice(g * heads_per_group * SSM_HD, (g + 1) * heads_per_group * SSM_HD)
        st_ref[g] = st * cd_x[:, gsl] + _dot(bg.T.astype(BF16), xgd[:, gsl])
        if not want_y:
            continue
        gm = lax.dot_general(cb, bb, (((1,), (1,)), ((), ())), preferred_element_type=F32)
        yoff = _dot(cb, st.astype(BF16))
        for jp in range(heads_per_group // 2):
            pair = g * (heads_per_group // 2) + jp
            sl = slice(pair * LANES, (pair + 1) * LANES)
            xg_pair = xg[:, sl]
            mds = []
            for hh in range(2):
                c = lane_lo + pair * 2 + hh
                seg = acs[:, c:c + 1] - acs_t[c:c + 1, :]
                decay = jnp.exp(jnp.where(tmask, seg, NEG))
                mds.append((gm * decay).astype(BF16))
            xg2 = jnp.concatenate([jnp.where(left, xg_pair, 0.0).astype(BF16),
                                   jnp.where(left, 0.0, xg_pair).astype(BF16)], axis=0)
            ys.append(ein_x[:, sl] * yoff[:, jp * LANES:(jp + 1) * LANES]
                      + _dot(jnp.concatenate(mds, axis=1), xg2))
    return (jnp.concatenate(ys, axis=1) if want_y else None), xs


def _ssd_fwd_body(actx_ref, actm_ref, dtx_ref, dtm_ref, tri_ref, e_ref, biasv_ref, alogv_ref,
                  yf_ref, st_ref):
    consts = (tri_ref[...], e_ref[...], biasv_ref[...], alogv_ref[...], 0, st_ref, True)

    @pl.when(pl.program_id(1) == 0)
    def _():
        st_ref[...] = jnp.zeros_like(st_ref)
        row = lax.broadcasted_iota(jnp.int32, (CHUNK, 1), 0)
        _ssd_chunk(actm_ref[0], dtm_ref[...], row >= NPAD, *consts, want_y=False)

    for sub in range(SSD_CHUNKS_PER_STEP):
        rows = slice(sub * CHUNK, (sub + 1) * CHUNK)
        y, _ = _ssd_chunk(actx_ref[rows, :], dtx_ref[rows, :], True, *consts)
        yf_ref[rows, :] = y


def _ssd_bwd_body(act_ref, dt_ref, yf_ref, z_ref, tri_ref, e_ref, biasv_ref, alogv_ref,
                  dskip_ref, nw_ref, out_ref, st_ref):
    @pl.when(pl.program_id(1) == 0)
    def _():
        st_ref[...] = jnp.zeros_like(st_ref)

    for sub in reversed(range(SSD_CHUNKS_PER_STEP)):
        rows = slice(sub * CHUNK, (sub + 1) * CHUNK)
        yb, xs = _ssd_chunk(act_ref[rows, :], dt_ref[rows, :], True, tri_ref[...], e_ref[...],
                            biasv_ref[...], alogv_ref[...], H_SSM, st_ref, False)
        y = yf_ref[rows, :] + yb + xs * dskip_ref[...]
        out_ref[rows, :] = _rms(y * _silu(z_ref[rows, :]), nw_ref[...]).astype(BF16)


def _ssd(act, meta_act, dt_raw, dt_meta, z, consts, nb, seq):
    step = SSD_CHUNKS_PER_STEP * CHUNK
    ns = seq // step
    tril, triu, e_f, e_b, biasv, alogv, dskip_x, ssm_nw = consts
    st_shape = pltpu.VMEM((SSM_G, SSM_N, (H_SSM // SSM_G) * SSM_HD), F32)

    xblk = lambda b, c: (b * ns + c, 0)
    y_f = pl.pallas_call(
        _ssd_fwd_body,
        out_shape=jax.ShapeDtypeStruct((nb * seq, D_SSM), F32),
        grid=(nb, ns),
        in_specs=[
            pl.BlockSpec((step, CONV_DIM), xblk),
            pl.BlockSpec((1, CHUNK, CONV_DIM), lambda b, c: (b, 0, 0)),
            pl.BlockSpec((step, LANES), xblk),
            _const_spec((CHUNK, LANES)),
            _const_spec((CHUNK, CHUNK)), _const_spec((LANES, D_SSM)),
            _const_spec((1, LANES)), _const_spec((1, LANES)),
        ],
        out_specs=pl.BlockSpec((step, D_SSM), xblk),
        scratch_shapes=[st_shape],
        compiler_params=_params(("arbitrary", "arbitrary")), name="ssd_fwd",
    )(act, meta_act, dt_raw, dt_meta, tril, e_f, biasv, alogv)

    rblk = lambda b, c: (b * ns + ns - 1 - c, 0)
    return pl.pallas_call(
        _ssd_bwd_body,
        out_shape=jax.ShapeDtypeStruct((nb * seq, D_SSM), BF16),
        grid=(nb, ns),
        in_specs=[
            pl.BlockSpec((step, CONV_DIM), rblk),
            pl.BlockSpec((step, LANES), rblk),
            pl.BlockSpec((step, D_SSM), rblk),
            pl.BlockSpec((step, D_SSM), rblk),
            _const_spec((CHUNK, CHUNK)), _const_spec((LANES, D_SSM)),
            _const_spec((1, LANES)), _const_spec((1, LANES)),
            _const_spec((1, D_SSM)), _const_spec((1, D_SSM)),
        ],
        out_specs=pl.BlockSpec((step, D_SSM), rblk),
        scratch_shapes=[st_shape],
        compiler_params=_params(("arbitrary", "arbitrary")), name="ssd_bwd",
    )(act, dt_raw, y_f, z, triu, e_b, biasv, alogv, dskip_x, ssm_nw)


def _attn_body(qT_ref, k_ref, vT_ref, km_ref, vmT_ref, slope_ref, lq1_ref, lk1_ref, lq2_ref,
               lk2_ref, nw_ref, out_ref, qaug_ref, sa_ref, sb_ref, acc_ref, dbias_ref, feat_ref,
               kfeat_ref,
               *, t, seq, unrolled):
    nkv = seq // t
    slope = slope_ref[0]
    slope_q = slope[:, 0:1]
    row = lax.broadcasted_iota(jnp.int32, (HEAD_W, 1), 0)

    def log2e_part(idx):
        part = idx % 3
        return jnp.where(part == 0, LOG2E_PARTS[0],
                         jnp.where(part == 1, LOG2E_PARTS[1], LOG2E_PARTS[2]))

    def key_feat(rows, rel, valid):
        lane = lax.broadcasted_iota(jnp.int32, (rows, LANES), 1)
        r_hi = (rel // BF16_EXACT_INTS) * BF16_EXACT_INTS
        f = jnp.where(lane < 3, slope * r_hi.astype(F32),
                      jnp.where(lane < 6, slope * (rel - r_hi).astype(F32),
                                jnp.where(lane < 12, log2e_part(lane), 0.0)))
        if valid is not None:
            f = jnp.where(valid, f, jnp.where(lane == 3, NEG, 0.0))
        return f.astype(BF16)

    @pl.when(jnp.logical_and(pl.program_id(1) == 0, pl.program_id(2) == 0))
    def _():
        frow = lax.broadcasted_iota(jnp.int32, (HEAD_W, t), 0)
        fcol = lax.broadcasted_iota(jnp.int32, (HEAD_W, t), 1)
        c_hi = (fcol // BF16_EXACT_INTS) * BF16_EXACT_INTS
        feat = jnp.where(frow < 6, log2e_part(frow),
                         jnp.where(frow < 9, -slope_q * c_hi.astype(F32),
                                   jnp.where(frow < 12, -slope_q * (fcol - c_hi).astype(F32), 0.0)))
        feat_ref[0] = feat.astype(BF16)
        feat_ref[1] = (-feat).astype(BF16)
        kfeat_ref[0:t, :] = key_feat(t, lax.broadcasted_iota(jnp.int32, (t, LANES), 0), None)
        mrow = lax.broadcasted_iota(jnp.int32, (CHUNK, LANES), 0)
        kfeat_ref[t:t + CHUNK, :] = key_feat(CHUNK, mrow - CHUNK, mrow >= NPAD)
        d = (lax.broadcasted_iota(jnp.int32, (t, t), 0)
             - lax.broadcasted_iota(jnp.int32, (t, t), 1))
        dbias_ref[...] = jnp.where(d > 0, (-2.0 * LOG2E * slope_q) * d.astype(F32), 0.0)

    def set_queries(qT, slot):
        zero = jnp.zeros_like(qT)
        qz = (jnp.where(row < ATT_DH, qT, zero), jnp.where(row >= ATT_DH, qT, zero))
        for sg in range(2):
            for mp in range(2):
                qaug_ref[slot, sg, mp] = jnp.concatenate([qz[mp], feat_ref[sg]], axis=0)

    ones_rows = jnp.ones((ONES_ROWS, t), BF16)
    ones_m = jnp.ones((ONES_ROWS, CHUNK), BF16)
    lam = (jnp.exp(jnp.sum(lq1_ref[...] * lk1_ref[...], axis=-1, keepdims=True))
           - jnp.exp(jnp.sum(lq2_ref[...] * lk2_ref[...], axis=-1, keepdims=True)) + LAM_INIT)
    slots = (sa_ref, sb_ref)

    def as_f32(x):
        return float(x) if isinstance(x, int) else x.astype(F32)

    def x_lhs(j):
        k0 = j * t if isinstance(j, int) else pl.multiple_of(j * t, t)
        return jnp.concatenate([k_ref[0, pl.ds(k0, t), :], kfeat_ref[0:t, :]], axis=1)

    def unit_shift(unit, i):
        if unit[0] == "x":
            d = unit[1] - i
            return (LOG2E * slope_q) * as_f32(-(abs(d) if isinstance(d, int) else jnp.abs(d)) * t)
        return (LOG2E * slope_q) * as_f32(-(i * t)) if unit[0] == "meta" else 0.0

    def unit_rows(unit):
        return CHUNK if unit[0] == "meta" else t

    def produce(unit, i, base, qslot, s_ref):
        if unit[0] == "x":
            j = unit[1]
            after = j > i
            sg = int(after) if isinstance(after, bool) else after.astype(jnp.int32)
            lhs = x_lhs(base + j)
            blocks = [_dot(lhs, qaug_ref[qslot, sg, mp]) for mp in range(2)]
        elif unit[0] == "meta":
            lhs = jnp.concatenate([km_ref[0], kfeat_ref[t:t + CHUNK, :]], axis=1)
            blocks = [_dot(lhs, qaug_ref[qslot, 0, mp]) for mp in range(2)]
        else:
            lhs = x_lhs(base + i)
            blocks = [_dot(lhs, qaug_ref[qslot, 0, mp]) + dbias_ref[...] for mp in range(2)]
        rows = unit_rows(unit)
        mbs = []
        for mp, s in enumerate(blocks):
            s_ref[mp, 0:rows, :] = s
            mbs.append(jnp.max(s, axis=0, keepdims=True))
        return tuple(mbs)

    def consume(unit, i, base, s_ref, mbs, ms):
        cterm = unit_shift(unit, i)
        rows = unit_rows(unit)
        if unit[0] == "meta":
            v_aug = jnp.concatenate([vmT_ref[0, 0], ones_m], axis=0)
        else:
            j = unit[1] if unit[0] == "x" else i
            v_aug = jnp.concatenate([vT_ref[0, base + j], ones_rows], axis=0)
        out = []
        for mp in range(2):
            m_new = jnp.maximum(ms[mp], mbs[mp] + cterm)
            p = jnp.exp2(s_ref[mp, 0:rows, :] - (m_new - cterm)).astype(BF16)
            alpha = jnp.exp2(ms[mp] - m_new)
            acc_ref[mp] = acc_ref[mp] * alpha + _dot(v_aug, p)
            out.append(m_new)
        return tuple(out)

    def start_tile():
        acc_ref[...] = jnp.zeros_like(acc_ref)
        return (jnp.full((1, t), NEG, F32),) * 2

    def finalize(rows):
        o0 = acc_ref[0, 0:HEAD_W, :] / acc_ref[0, HEAD_W:HEAD_W + 1, :]
        o1 = acc_ref[1, 0:HEAD_W, :] / acc_ref[1, HEAD_W:HEAD_W + 1, :]
        o = o0 - lam * o1
        on = o * lax.rsqrt(jnp.mean(o * o, axis=0, keepdims=True) + EPS)
        out_ref[rows, :] = ((on.T * nw_ref[...]) * (1.0 - LAM_INIT)).astype(BF16)

    def run_static(units, pending, ms):
        for u, (l, i, base, unit) in enumerate(units):
            nxt = None
            if u + 1 < len(units):
                nl, ni, nbase, nunit = units[u + 1]
                if nl != l:
                    set_queries(qT_ref[0, :, nl * t:(nl + 1) * t], nl % 2)
                nxt = produce(nunit, ni, nbase, nl % 2, slots[(u + 1) % 2])
            if unrolled and (u == 0 or units[u - 1][0] != l):
                ms = start_tile()
            ms = consume(unit, i, base, slots[u % 2], pending, ms)
            pending = nxt
            if unrolled and (u + 1 == len(units) or units[u + 1][0] != l):
                finalize(slice(l * t, (l + 1) * t))
        return ms

    def x_units(i):
        return [("x", n + (n >= i)) for n in range(nkv - 1)]

    tail = [("diag",), ("meta",)]
    if unrolled:
        units = []
        for l in range(unrolled):
            i, base = l % nkv, (l // nkv) * nkv
            units += [(l, i, base, u) for u in x_units(i) + tail]
        set_queries(qT_ref[0, :, 0:t], 0)
        run_static(units, produce(units[0][3], 0, 0, 0, sa_ref), None)
    else:
        i = pl.program_id(2)
        set_queries(qT_ref[0], 0)
        ms = start_tile()
        n_off = nkv - 1
        xunit = lambda n: ("x", n + (n >= i).astype(jnp.int32))
        rest = tail
        if n_off >= 1:
            mbs = produce(xunit(0), i, 0, 0, sa_ref)
            pairs = (n_off - 1) // 2

            def pair_body(u, carry):
                ms, mbs = carry[:2], carry[2:]
                n = 2 * u
                nxt = produce(xunit(n + 1), i, 0, 0, sb_ref)
                ms = consume(xunit(n), i, 0, sa_ref, mbs, ms)
                mbs = produce(xunit(n + 2), i, 0, 0, sa_ref)
                ms = consume(xunit(n + 1), i, 0, sb_ref, nxt, ms)
                return ms + mbs

            unroll = next((f for f in (5, 3) if pairs % f == 0), 1)
            carry = lax.fori_loop(0, pairs, pair_body, ms + mbs, unroll=unroll)
            ms, mbs = carry[:2], carry[2:]
            rest = [xunit(n) for n in range(2 * pairs, n_off)] + tail
        else:
            mbs = produce(rest[0], i, 0, 0, sa_ref)
        run_static([(0, i, 0, u) for u in rest], mbs, ms)
        finalize(slice(None))


def _attention(qT, k, vT, k_meta, vT_meta, slopes, lq1, lk1, lq2, lk2, attn_nw, nb, seq):
    t = min(ROW_TILE, seq)
    nq = seq // t
    unrolled, group = 0, 1
    if nq * (nq + 1) <= ATTN_UNROLL_MAX_UNITS:
        group = ATTN_UNROLL_SEQS if nb % ATTN_UNROLL_SEQS == 0 else 1
        unrolled = group * nq
    tq = max(unrolled, 1) * t
    steps = (group * seq) // tq
    grid = (N_HEADS, nb // group, steps)
    q_map = lambda h, b, i: (h, 0, b * steps + i)
    out_map = lambda h, b, i: (b * steps + i, h)
    per_seq = lambda f: (lambda h, b, i: f(h, b))
    vec = lambda n: _const_spec((1, n))
    in_specs = [
        pl.BlockSpec((1, HEAD_W, tq), q_map),
        pl.BlockSpec((1, group * seq, HEAD_W), per_seq(lambda h, b: (h, b, 0))),
        pl.BlockSpec((1, group * nq, HEAD_W, t), per_seq(lambda h, b: (h, b, 0, 0))),
        pl.BlockSpec((1, CHUNK, HEAD_W), per_seq(lambda h, b: (h, 0, 0))),
        pl.BlockSpec((1, 1, HEAD_W, CHUNK), per_seq(lambda h, b: (h, 0, 0, 0))),
        pl.BlockSpec((1, 1, LANES), per_seq(lambda h, b: (h, 0, 0))),
        vec(ATT_DH), vec(ATT_DH), vec(ATT_DH), vec(ATT_DH), vec(HEAD_W),
    ]
    return pl.pallas_call(
        functools.partial(_attn_body, t=t, seq=seq, unrolled=unrolled),
        out_shape=jax.ShapeDtypeStruct((nb * seq, ATT_VDIM), BF16),
        grid=grid, in_specs=in_specs,
        out_specs=pl.BlockSpec((tq, HEAD_W), out_map),
        scratch_shapes=[pltpu.VMEM((2, 2, 2, 2 * HEAD_W, t), BF16),
                        pltpu.VMEM((2, t, t), F32), pltpu.VMEM((2, t, t), F32),
                        pltpu.VMEM((2, HEAD_W + ONES_ROWS, t), F32),
                        pltpu.VMEM((t, t), F32),
                        pltpu.VMEM((2, HEAD_W, t), BF16),
                        pltpu.VMEM((t + CHUNK, LANES), BF16)],
        compiler_params=_params(("arbitrary",) * len(grid)), name="attn",
    )(qT, k, vT, k_meta, vT_meta, slopes, lq1, lk1, lq2, lk2, attn_nw)


def _mlp_body(x_ref, att_ref, ssm_ref, wo_ref, n2_ref, wup_ref, wdown_ref, fw_ref, out_ref):
    h1 = (x_ref[...] + _dot(att_ref[...], wo_ref[0:ATT_VDIM, :])
          + _dot(ssm_ref[...], wo_ref[ATT_VDIM:ATT_VDIM + D_SSM, :]))
    u2 = _rms(h1, n2_ref[...]).astype(BF16)
    hid = jnp.maximum(_dot(u2, wup_ref[...]), 0.0)
    h2 = h1 + _dot((hid * hid).astype(BF16), wdown_ref[...])
    out_ref[...] = _rms(h2, fw_ref[...])


def _mlp(x2d, att, ssm, wo, n2, wup, wdown, fw, tm):
    rows = x2d.shape[0]
    tile = lambda w: pl.BlockSpec((tm, w), lambda i: (i, 0))
    return pl.pallas_call(
        _mlp_body, out_shape=jax.ShapeDtypeStruct((rows, D_MODEL), F32), grid=(rows // tm,),
        in_specs=[tile(D_MODEL), tile(ATT_VDIM), tile(D_SSM), _const_spec(wo.shape),
                  _const_spec((1, D_MODEL)), _const_spec(wup.shape), _const_spec(wdown.shape),
                  _const_spec((1, D_MODEL))],
        out_specs=tile(D_MODEL),
        compiler_params=_params(("arbitrary",)), name="mlp",
    )(x2d, att, ssm, wo, n2, wup, wdown, fw)


def _head_lane_vec(fwd_vals, bwd_vals):
    v = jnp.zeros((1, LANES), F32)
    v = v.at[0, 0:H_SSM].set(fwd_vals.astype(F32))
    return v.at[0, H_SSM:2 * H_SSM].set(bwd_vals.astype(F32))


def _expand_matrix(lane_lo):
    lane = jnp.arange(LANES)[:, None]
    ch = jnp.arange(D_SSM)[None, :] // SSM_HD
    return (lane == ch + lane_lo).astype(BF16)


def _trunk(x, meta_parts, p):
    nb, seq, _ = x.shape
    x2d = x.reshape(nb * seq, D_MODEL)
    k_meta, vT_meta, xbc_meta, dt_meta = meta_parts
    tm = min(ROW_TILE, seq)
    qT, k, vT, z, act, dt_raw, meta_act = _inproj(
        x2d, xbc_meta, p["n1"], p["w_in"], p["conv_w8"], p["conv_b"], nb, seq, tm)
    ssm = _ssd(act, meta_act, dt_raw, dt_meta, z, p["ssd_consts"], nb, seq)
    att = _attention(qT, k, vT, k_meta, vT_meta, p["slopes"], p["lq1"], p["lk1"], p["lq2"],
                     p["lk2"], p["attn_nw"], nb, seq)
    y = _mlp(x2d, att, ssm, p["wo"], p["n2"], p["wup"], p["wdown"], p["fw"], tm)
    return y.reshape(nb, seq, D_MODEL)


@jax.jit
def kernel(x_prompt, x_sample, meta_tokens, norm1_w, w_in, conv_w, conv_b, lambda_q1, lambda_k1,
           lambda_q2, lambda_k2, attn_norm_w, dt_bias_f, dt_bias_b, a_log_f, a_log_b, d_skip,
           ssm_norm_w, w_out, norm2_w, w_up, w_down, final_norm_w):
    li = 0
    w = w_in[li].astype(BF16)
    cuts = (0, ATT_VDIM, 2 * ATT_VDIM, 3 * ATT_VDIM, 3 * ATT_VDIM + D_SSM,
            3 * ATT_VDIM + D_SSM + CONV_DIM)
    wdt = jnp.zeros((D_MODEL, LANES), BF16).at[:, 0:2 * H_SSM].set(w[:, cuts[5]:])
    t = jnp.arange(CHUNK)
    p = {
        "n1": norm1_w[li].reshape(1, D_MODEL).astype(F32),
        "w_in": tuple(w[:, cuts[j]:cuts[j + 1]] for j in range(5)) + (wdt,),
        "conv_w8": jnp.zeros((HALO, CONV_DIM), F32).at[0:D_CONV].set(conv_w[li].astype(F32)),
        "conv_b": conv_b[li].reshape(1, CONV_DIM).astype(F32),
        "ssd_consts": (
            (t[:, None] >= t[None, :]).astype(BF16),
            (t[None, :] >= t[:, None]).astype(BF16),
            _expand_matrix(0), _expand_matrix(H_SSM),
            _head_lane_vec(dt_bias_f[li], dt_bias_b[li]),
            _head_lane_vec(a_log_f[li], a_log_b[li]),
            jnp.repeat(d_skip[li].astype(F32), SSM_HD).reshape(1, D_SSM),
            ssm_norm_w[li].reshape(1, D_SSM).astype(F32),
        ),
        "slopes": jnp.broadcast_to(
            (2.0 ** (-8.0 * (jnp.arange(N_HEADS, dtype=F32) + 1.0) / N_HEADS))[:, None, None],
            (N_HEADS, 1, LANES)),
        "lq1": lambda_q1[li].reshape(1, ATT_DH).astype(F32),
        "lk1": lambda_k1[li].reshape(1, ATT_DH).astype(F32),
        "lq2": lambda_q2[li].reshape(1, ATT_DH).astype(F32),
        "lk2": lambda_k2[li].reshape(1, ATT_DH).astype(F32),
        "attn_nw": attn_norm_w[li].reshape(1, HEAD_W).astype(F32),
        "wo": w_out[li].astype(BF16),
        "n2": norm2_w[li].reshape(1, D_MODEL).astype(F32),
        "wup": w_up[li].astype(BF16),
        "wdown": w_down[li].astype(BF16),
        "fw": final_norm_w.reshape(1, D_MODEL).astype(F32),
    }
    meta_block = jnp.concatenate(
        [jnp.zeros((NPAD, D_MODEL), F32), meta_tokens.astype(F32)], axis=0)
    _, k_m, vT_m, _, xbc_m, dt_m = _inproj_meta(meta_block, p["n1"], p["w_in"])
    meta_parts = (k_m, vT_m, xbc_m, dt_m)
    return (_trunk(x_prompt, meta_parts, p), _trunk(x_sample, meta_parts, p))
```
